```python
import math
import jax, jax.numpy as jnp
from jax import lax
import numpy as np

D_MODEL = 1024
BATCH = 4
SEQ = 8192
DEPTH = 2

A_HEADS = 4
A_QK_DIM = 64
A_V_DIM = 2 * A_QK_DIM
A_QK_COLS = A_HEADS * 2 * A_QK_DIM
A_WIDTH = A_HEADS * A_V_DIM
ROPE_THETA = 500000.0
ROPE_DIM = A_QK_DIM // 4
Q_BLOCK = 128
NEG_INF = -1e30
C_WIDTH = 512
CONV_WIDTH = 3
R_HEADS = 4
R_QK_DIM = 64
R_V_DIM = 2 * R_QK_DIM
R_QK_COLS = R_HEADS * R_QK_DIM
R_WIDTH = R_HEADS * R_V_DIM
R_CHUNK = 128
RET_THETA = 10000.0
N_BRANCH = 3
BRANCH_WIDTH = 512
EPS = 1e-6

SPLIT_SIZES = (A_QK_COLS, A_QK_COLS, A_WIDTH, A_WIDTH,
               C_WIDTH, C_WIDTH, C_WIDTH, C_WIDTH,
               R_QK_COLS, R_QK_COLS, R_WIDTH, R_WIDTH,
               N_BRANCH * D_MODEL)
IN_COLS = sum(SPLIT_SIZES)

kernel_name = "gated_parallel_diffattn_shortconv_retention"


def rms_norm(x, g=None):
    xf = x.astype(jnp.float32)
    y = xf * lax.rsqrt(jnp.mean(xf * xf, axis=-1, keepdims=True) + EPS)
    return y if g is None else y * g.astype(jnp.float32)


def rotate(x, cos, sin):
    x1, x2 = jnp.split(x, 2, axis=-1)
    return jnp.concatenate([x1 * cos - x2 * sin, x2 * cos + x1 * sin], axis=-1)


def diff_attention(q, k, v, lam, subln_g, lam_init):
    b, s = q.shape[0], q.shape[1]
    pos = jnp.arange(s, dtype=jnp.float32)
    inv = ROPE_THETA ** (-jnp.arange(0, ROPE_DIM, 2, dtype=jnp.float32) / ROPE_DIM)
    ang = pos[:, None] * inv[None, :]
    cos = jnp.cos(ang)[:, None, None, :]
    sin = jnp.sin(ang)[:, None, None, :]
    q = jnp.concatenate([rotate(q[..., :ROPE_DIM], cos, sin), q[..., ROPE_DIM:]], axis=-1)
    k = jnp.concatenate([rotate(k[..., :ROPE_DIM], cos, sin), k[..., ROPE_DIM:]], axis=-1)
    q = q * (A_QK_DIM ** -0.5)
    kh = k.transpose(0, 2, 3, 1, 4)
    vh = v.transpose(0, 2, 1, 3)
    nb = s // Q_BLOCK
    qb = q.transpose(0, 2, 3, 1, 4).reshape(b, A_HEADS, 2, nb, Q_BLOCK, A_QK_DIM)
    qb = qb.transpose(3, 0, 1, 2, 4, 5)
    kpos = jnp.arange(s)

    def block(args):
        qi, start = args
        sc = jnp.einsum('bhcqd,bhckd->bhcqk', qi, kh).astype(jnp.float32)
        qpos = start + jnp.arange(Q_BLOCK)
        mask = kpos[None, :] <= qpos[:, None]
        p = jax.nn.softmax(jnp.where(mask, sc, NEG_INF), axis=-1)
        w = p[:, :, 0] - lam * p[:, :, 1]
        return jnp.einsum('bhqk,bhkv->bhqv', w, vh)

    out = lax.map(block, (qb, jnp.arange(nb, dtype=jnp.int32) * Q_BLOCK))
    out = out.transpose(1, 0, 3, 2, 4).reshape(b, s, A_HEADS, A_V_DIM)
    out = rms_norm(out, subln_g) * (1.0 - lam_init)
    return out.reshape(b, s, A_WIDTH)


def short_conv(x_in, gate_b, gate_c, w):
    u = gate_c * x_in
    y = lax.conv_general_dilated(u, w[:, None, :].astype(u.dtype), window_strides=(1,),
                                 padding=[(CONV_WIDTH - 1, 0)],
                                 dimension_numbers=('NWC', 'WIO', 'NWC'),
                                 feature_group_count=C_WIDTH)
    return gate_b * y


def retention(q, k, v):
    b, s = q.shape[0], q.shape[1]
    pos = jnp.arange(s, dtype=jnp.float32)
    inv = 1.0 / (RET_THETA ** jnp.linspace(0.0, 1.0, R_QK_DIM // 2, dtype=jnp.float32))
    ang = pos[:, None] * inv[None, :]
    cos = jnp.cos(ang)[:, None, :]
    sin = jnp.sin(ang)[:, None, :]
    q = rotate(q, cos, sin)
    k = rotate(k, cos, sin) * (R_QK_DIM ** -0.5)
    log_g = jnp.log(1.0 - 2.0 ** (-5.0 - jnp.arange(R_HEADS, dtype=jnp.float32)))
    nc = s // R_CHUNK

    def chunks(t):
        return t.reshape(b, nc, R_CHUNK, R_HEADS, t.shape[-1]).transpose(0, 3, 1, 2, 4)

    qc, kc, vc = chunks(q), chunks(k), chunks(v)
    idx = jnp.arange(R_CHUNK, dtype=jnp.float32)
    diff = idx[:, None] - idx[None, :]
    dmask = jnp.where(diff >= 0,
                      jnp.exp(jnp.where(diff >= 0, diff, 0.0)[None] * log_g[:, None, None]),
                      0.0)
    inner = jnp.einsum('bhncd,bhnmd->bhncm', qc, kc) * dmask[None, :, None]
    inner = jnp.einsum('bhncm,bhnme->bhnce', inner, vc)
    zeta = jnp.exp((R_CHUNK - 1 - idx)[None, :] * log_g[:, None])
    kv = jnp.einsum('bhnmd,bhnme->bhnde', kc * zeta[None, :, None, :, None], vc)
    chunk_decay = jnp.exp(R_CHUNK * log_g)[None, :, None, None]

    def step(state, kv_n):
        return (chunk_decay * state + kv_n).astype(kv_n.dtype), state

    init = jnp.zeros((b, R_HEADS, R_QK_DIM, R_V_DIM), kv.dtype)
    _, prev = lax.scan(step, init, kv.transpose(2, 0, 1, 3, 4))
    prev = prev.transpose(1, 2, 0, 3, 4)
    xi = jnp.exp((idx + 1.0)[None, :] * log_g[:, None])
    cross = jnp.einsum('bhncd,bhnde->bhnce', qc, prev) * xi[None, :, None, :, None]
    out = (inner + cross).transpose(0, 2, 3, 1, 4).reshape(b, s, R_HEADS, R_V_DIM)
    out = rms_norm(out)
    return out.reshape(b, s, R_WIDTH)


def hybrid_layer(x, norm_g, w_in, attn_lambda, attn_subln_g, conv_w, w_branch, w_out, layer):
    b, s, _ = x.shape
    h = rms_norm(x, norm_g)
    proj = h @ w_in
    split_points = [int(p) for p in np.cumsum(SPLIT_SIZES)[:-1]]
    (aq, ak, av, az, cx, cb, cc, cz, rq, rk, rv, rz, gates) = jnp.split(proj, split_points, axis=-1)

    lam_init = 0.8 - 0.6 * math.exp(-0.3 * layer)
    lp = attn_lambda.astype(jnp.float32)
    lam = jnp.exp(jnp.sum(lp[0] * lp[1])) - jnp.exp(jnp.sum(lp[2] * lp[3])) + lam_init
    a = diff_attention(aq.reshape(b, s, A_HEADS, 2, A_QK_DIM),
                       ak.reshape(b, s, A_HEADS, 2, A_QK_DIM),
                       av.reshape(b, s, A_HEADS, A_V_DIM),
                       lam, attn_subln_g, lam_init) * jax.nn.silu(az)
    c = short_conv(cx, cb, cc, conv_w) * jax.nn.silu(cz)
    r = retention(rq.reshape(b, s, R_HEADS, R_QK_DIM),
                  rk.reshape(b, s, R_HEADS, R_QK_DIM),
                  rv.reshape(b, s, R_HEADS, R_V_DIM)) * jax.nn.silu(rz)

    g = jax.nn.sigmoid(gates.astype(jnp.float32)).reshape(b, s, N_BRANCH, D_MODEL)
    branches = (a, c, r)
    merged = g[:, :, 0] * (branches[0] @ w_branch[0])
    for i in range(1, N_BRANCH):
        merged = merged + g[:, :, i] * (branches[i] @ w_branch[i])
    return x + (merged @ w_out).astype(x.dtype)


def setup_inputs(seed: int = 0) -> dict:
    key = jax.random.key(seed)
    ks = jax.random.split(key, 10)
    f32 = jnp.float32
    x = jax.random.normal(ks[0], (BATCH, SEQ, D_MODEL), f32)
    norm_g = 1.0 + 0.01 * jax.random.normal(ks[1], (DEPTH, D_MODEL), f32)
    w_in = jax.random.normal(ks[2], (DEPTH, D_MODEL, IN_COLS), f32) * D_MODEL ** -0.5
    attn_lambda = 0.1 * jax.random.normal(ks[3], (DEPTH, 4, A_QK_DIM), f32)
    attn_subln_g = 1.0 + 0.01 * jax.random.normal(ks[4], (DEPTH, A_V_DIM), f32)
    conv_w = jax.random.normal(ks[5], (DEPTH, CONV_WIDTH, C_WIDTH), f32) * CONV_WIDTH ** -0.5
    w_branch = jax.random.normal(ks[6], (DEPTH, N_BRANCH, BRANCH_WIDTH, D_MODEL), f32) * BRANCH_WIDTH ** -0.5
    w_out = jax.random.normal(ks[7], (DEPTH, D_MODEL, D_MODEL), f32) * D_MODEL ** -0.5
    final_norm_g = 1.0 + 0.01 * jax.random.normal(ks[8], (D_MODEL,), f32)
    return {"x": x, "norm_g": norm_g, "w_in": w_in, "attn_lambda": attn_lambda,
            "attn_subln_g": attn_subln_g, "conv_w": conv_w, "w_branch": w_branch,
            "w_out": w_out, "final_norm_g": final_norm_g}


def reference(x, norm_g, w_in, attn_lambda, attn_subln_g, conv_w, w_branch, w_out, final_norm_g):
    for layer in range(DEPTH):
        x = hybrid_layer(x, norm_g[layer], w_in[layer], attn_lambda[layer], attn_subln_g[layer],
                         conv_w[layer], w_branch[layer], w_out[layer], layer)
    return rms_norm(x, final_norm_g).astype(x.dtype)
```

```python
import functools
import math

import jax
import jax.numpy as jnp
from jax import lax
from jax.experimental import pallas as pl
from jax.experimental.pallas import tpu as pltpu

F32 = jnp.float32
BF16 = jnp.bfloat16

A_HEADS = 4
A_QK_DIM = 64
ROPE_THETA = 500000.0
ROPE_DIM = A_QK_DIM // 4
NEG_INF = -1e30
CONV_WIDTH = 3
R_HEADS = 4
R_QK_DIM = 64
RET_THETA = 10000.0
N_BRANCH = 3
EPS = 1e-6

LANES = 128
GROUP = 512
VMEM_LIMIT = 56 * 1024 * 1024

PROJ_ROWS = 512
ATT_TILE = 512
RET_CHUNK = 512


def _silu(t):
    return t * (1.0 / (1.0 + jnp.exp(-t)))


def _rms(x):
    return x * lax.rsqrt(jnp.mean(x * x, axis=-1, keepdims=True) + EPS)


def _rope_block(blk, c, s_up, s_dn, shift):
    return (blk * c + pltpu.roll(blk, LANES - shift, 1) * s_up
            + pltpu.roll(blk, shift, 1) * s_dn)


def _inproj_kernel(x_ref, g_ref, w_ref, atab_ref, rtab_ref, cw_ref,
                   aq_ref, ak_ref, av_ref, az_ref, c_ref,
                   rq_ref, rk_ref, rv_ref, rz_ref, ubuf_ref, *, tiles_per_seq):
    i = pl.program_id(0)
    tm = x_ref.shape[0]
    h = (_rms(x_ref[...]) * g_ref[...]).astype(BF16)

    def proj(gi):
        return jnp.dot(h, w_ref[:, gi * GROUP:(gi + 1) * GROUP], preferred_element_type=F32)

    for gi, out_ref, t0 in ((0, aq_ref, 0), (1, ak_ref, 3 * LANES)):
        y = proj(gi)
        c = atab_ref[:, t0:t0 + LANES]
        s_up = atab_ref[:, t0 + LANES:t0 + 2 * LANES]
        s_dn = atab_ref[:, t0 + 2 * LANES:t0 + 3 * LANES]
        for hh in range(A_HEADS):
            sl = slice(hh * LANES, (hh + 1) * LANES)
            out_ref[:, sl] = _rope_block(y[:, sl], c, s_up, s_dn, ROPE_DIM // 2).astype(BF16)
    av_ref[...] = proj(2).astype(BF16)
    az_ref[...] = _silu(proj(3)).astype(BF16)

    u = proj(4) * proj(6)

    @pl.when(i % tiles_per_seq == 0)
    def _():
        ubuf_ref[0:8, :] = jnp.zeros((8, GROUP), F32)

    ubuf_ref[8:8 + tm, :] = u
    conv = (cw_ref[0:1, :] * ubuf_ref[6:6 + tm, :] + cw_ref[1:2, :] * ubuf_ref[7:7 + tm, :]
            + cw_ref[2:3, :] * u)
    ubuf_ref[0:8, :] = u[tm - 8:tm, :]
    c_ref[...] = (proj(5) * conv * _silu(proj(7))).astype(BF16)

    y = proj(8)
    c = rtab_ref[:, 0:LANES]
    s_up = rtab_ref[:, LANES:2 * LANES]
    s_dn = rtab_ref[:, 2 * LANES:3 * LANES]
    lane = lax.broadcasted_iota(jnp.int32, (tm, LANES), 1)
    low = lane < R_QK_DIM
    for hh in range(R_HEADS):
        sl = slice(hh * LANES, (hh + 1) * LANES)
        rot = _rope_block(y[:, sl], c, s_up, s_dn, R_QK_DIM // 2)
        rq_ref[:, sl] = jnp.where(low, rot, 0.0).astype(BF16)
        rk_ref[:, sl] = jnp.where(low, pltpu.roll(rot, R_QK_DIM, 1), 0.0).astype(BF16)
    rv_ref[...] = proj(9).astype(BF16)
    rz_ref[...] = _silu(proj(10)).astype(BF16)


def _inproj(x2, norm_g, w1, atab, rtab, conv_w, seq):
    n, d = x2.shape
    tm = PROJ_ROWS
    tiles_per_seq = seq // tm
    row = lambda i: (i, 0)
    pos = lambda i: (i % tiles_per_seq, 0)
    const = lambda i: (0, 0)
    out_sds = jax.ShapeDtypeStruct((n, GROUP), BF16)
    out_spec = pl.BlockSpec((tm, GROUP), row)
    return pl.pallas_call(
        functools.partial(_inproj_kernel, tiles_per_seq=tiles_per_seq),
        grid=(n // tm,),
        in_specs=[
            pl.BlockSpec((tm, d), row),
            pl.BlockSpec((1, d), const),
            pl.BlockSpec(w1.shape, const),
            pl.BlockSpec((tm, atab.shape[1]), pos),
            pl.BlockSpec((tm, rtab.shape[1]), pos),
            pl.BlockSpec(conv_w.shape, const),
        ],
        out_specs=[out_spec] * 9,
        out_shape=[out_sds] * 9,
        scratch_shapes=[pltpu.VMEM((tm + 8, GROUP), F32)],
        compiler_params=pltpu.CompilerParams(
            dimension_semantics=("arbitrary",), vmem_limit_bytes=VMEM_LIMIT),
        name="inproj",
    )(x2, norm_g, w1, atab, rtab, conv_w)


def _attn_kernel(lam_ref, g_ref, q_ref, k_ref, v_ref, z_ref, o_ref,
                 m_ref, l_ref, acc_ref, *, lam_init):
    i = pl.program_id(2)
    tq = q_ref.shape[0]
    q = q_ref[...]
    lane = lax.broadcasted_iota(jnp.int32, q.shape, 1)
    zero = jnp.zeros_like(q)
    qs = jnp.concatenate([jnp.where(lane < A_QK_DIM, q, zero),
                          jnp.where(lane >= A_QK_DIM, q, zero)], axis=0)
    m_ref[...] = jnp.full(m_ref.shape, NEG_INF, F32)
    l_ref[...] = jnp.zeros(l_ref.shape, F32)
    acc_ref[...] = jnp.zeros(acc_ref.shape, F32)

    def step(j, masked):
        off = pl.multiple_of(j * tq, tq)
        ks = k_ref[pl.ds(off, tq), :]
        vs = v_ref[pl.ds(off, tq), :]
        s = lax.dot_general(qs, ks, (((1,), (1,)), ((), ())), preferred_element_type=F32)
        if masked:
            r = lax.broadcasted_iota(jnp.int32, s.shape, 0)
            cidx = lax.broadcasted_iota(jnp.int32, s.shape, 1)
            s = jnp.where(cidx <= jnp.where(r >= tq, r - tq, r), s, NEG_INF)
        m_prev = m_ref[...]
        m_new = jnp.maximum(m_prev, jnp.max(s, axis=1, keepdims=True))
        alpha = jnp.exp(m_prev - m_new)
        p = jnp.exp(s - m_new)
        l_ref[...] = alpha * l_ref[...] + jnp.sum(p, axis=1, keepdims=True)
        acc_ref[...] = alpha * acc_ref[...] + jnp.dot(p.astype(BF16), vs,
                                                       preferred_element_type=F32)
        m_ref[...] = m_new

    def body(j, carry):
        step(j, False)
        return carry

    lax.fori_loop(0, i, body, 0)
    step(i, True)

    o = acc_ref[...] / l_ref[...]
    al = lam_ref[...]
    lam = (jnp.exp(jnp.sum(al[0:1] * al[1:2], axis=1, keepdims=True))
           - jnp.exp(jnp.sum(al[2:3] * al[3:4], axis=1, keepdims=True)) + lam_init)
    d = o[:tq] - lam * o[tq:]
    d = _rms(d) * g_ref[...] * (1.0 - lam_init)
    o_ref[...] = (d * z_ref[...].astype(F32)).astype(BF16)


def _attention(aq, ak, av, az, attn_lambda, subln_g, lam_init):
    b, s, _ = aq.shape
    tq = ATT_TILE
    tile = pl.BlockSpec((None, tq, LANES), lambda bi, hi, qi: (bi, qi, hi))
    full = pl.BlockSpec((None, s, LANES), lambda bi, hi, qi: (bi, 0, hi))
    const = lambda bi, hi, qi: (0, 0)
    return pl.pallas_call(
        functools.partial(_attn_kernel, lam_init=lam_init),
        grid=(b, A_HEADS, s // tq),
        in_specs=[
            pl.BlockSpec(attn_lambda.shape, const),
            pl.BlockSpec(subln_g.shape, const),
            tile, full, full, tile,
        ],
        out_specs=tile,
        out_shape=jax.ShapeDtypeStruct(aq.shape, BF16),
        scratch_shapes=[pltpu.VMEM((2 * tq, 1), F32), pltpu.VMEM((2 * tq, 1), F32),
                        pltpu.VMEM((2 * tq, LANES), F32)],
        compiler_params=pltpu.CompilerParams(
            dimension_semantics=("arbitrary", "arbitrary", "arbitrary"),
            vmem_limit_bytes=VMEM_LIMIT),
        name="diff_attention",
    )(attn_lambda, subln_g, aq, ak, av, az)


def _ret_kernel(q_ref, k_ref, v_ref, z_ref, dm_ref, xi_ref, zeta_ref, dec_ref, o_ref, st_ref):
    @pl.when(pl.program_id(1) == 0)
    def _():
        st_ref[...] = jnp.zeros(st_ref.shape, F32)

    for hh in range(R_HEADS):
        sl = slice(hh * LANES, (hh + 1) * LANES)
        q = q_ref[:, sl]
        k = k_ref[:, sl]
        v = v_ref[:, sl]
        s = lax.dot_general(q, k, (((1,), (1,)), ((), ())), preferred_element_type=F32)
        inner = jnp.dot((s * dm_ref[hh]).astype(BF16), v, preferred_element_type=F32)
        st = st_ref[hh]
        cross = jnp.dot(q, st.astype(BF16), preferred_element_type=F32) * xi_ref[hh]
        o = _rms(inner + cross)
        o_ref[:, sl] = (o * z_ref[:, sl].astype(F32)).astype(BF16)
        kz = (k.astype(F32) * zeta_ref[hh]).astype(BF16)
        kv = lax.dot_general(kz, v, (((0,), (0,)), ((), ())), preferred_element_type=F32)
        st_ref[hh] = dec_ref[hh] * st + kv


def _retention(rq, rk, rv, rz, dmask, xi, zeta, dec):
    b, s, w = rq.shape
    c = RET_CHUNK
    tile = pl.BlockSpec((None, c, w), lambda bi, ti: (bi, ti, 0))
    const3 = lambda bi, ti: (0, 0, 0)
    return pl.pallas_call(
        _ret_kernel,
        grid=(b, s // c),
        in_specs=[tile, tile, tile, tile,
                  pl.BlockSpec(dmask.shape, const3), pl.BlockSpec(xi.shape, const3),
                  pl.BlockSpec(zeta.shape, const3), pl.BlockSpec(dec.shape, const3)],
        out_specs=tile,
        out_shape=jax.ShapeDtypeStruct(rq.shape, BF16),
        scratch_shapes=[pltpu.VMEM((R_HEADS, LANES, LANES), F32)],
        compiler_params=pltpu.CompilerParams(
            dimension_semantics=("arbitrary", "arbitrary"), vmem_limit_bytes=VMEM_LIMIT),
        name="retention",
    )(rq, rk, rv, rz, dmask, xi, zeta, dec)


def _merge_kernel(x_ref, a_ref, c_ref, r_ref, g_ref, wg_ref, wb_ref, wo_ref, fg_ref, o_ref,
                  *, final_norm):
    x = x_ref[...]
    d = x.shape[1]
    h = (_rms(x) * g_ref[...]).astype(BF16)
    merged = None
    for bi, br_ref in enumerate((a_ref, c_ref, r_ref)):
        t = jnp.dot(h, wg_ref[:, bi * d:(bi + 1) * d], preferred_element_type=F32)
        gate = 1.0 / (1.0 + jnp.exp(-t))
        contrib = gate * jnp.dot(br_ref[...], wb_ref[bi], preferred_element_type=F32)
        merged = contrib if merged is None else merged + contrib
    y = x + jnp.dot(merged.astype(BF16), wo_ref[...], preferred_element_type=F32)
    if final_norm:
        y = _rms(y) * fg_ref[...]
    o_ref[...] = y


def _merge(x2, a, c, r, norm_g, wg, wb, wo, final_g, final_norm):
    n, d = x2.shape
    tm = PROJ_ROWS
    row = lambda i: (i, 0)
    const = lambda i: (0, 0)
    br = pl.BlockSpec((tm, GROUP), row)
    return pl.pallas_call(
        functools.partial(_merge_kernel, final_norm=final_norm),
        grid=(n // tm,),
        in_specs=[pl.BlockSpec((tm, d), row), br, br, br,
                  pl.BlockSpec((1, d), const),
                  pl.BlockSpec(wg.shape, const),
                  pl.BlockSpec(wb.shape, lambda i: (0, 0, 0)),
                  pl.BlockSpec(wo.shape, const),
                  pl.BlockSpec((1, d), const)],
        out_specs=pl.BlockSpec((tm, d), row),
        out_shape=jax.ShapeDtypeStruct((n, d), F32),
        compiler_params=pltpu.CompilerParams(
            dimension_semantics=("arbitrary",), vmem_limit_bytes=VMEM_LIMIT),
        name="merge_outproj",
    )(x2, a, c, r, norm_g, wg, wb, wo, final_g)


def _attn_rope_table(seq):
    pos = jnp.arange(seq, dtype=F32)
    inv = ROPE_THETA ** (-jnp.arange(0, ROPE_DIM, 2, dtype=F32) / ROPE_DIM)
    ang = pos[:, None] * inv[None, :]
    cos, sin = jnp.cos(ang), jnp.sin(ang)
    half = ROPE_DIM // 2
    dd = jnp.arange(LANES) % A_QK_DIM
    j = dd % half
    c = jnp.where(dd[None, :] < ROPE_DIM, cos[:, j], 1.0)
    s_up = jnp.where(dd[None, :] < half, -sin[:, j], 0.0)
    s_dn = jnp.where((dd[None, :] >= half) & (dd[None, :] < ROPE_DIM), sin[:, j], 0.0)
    k_tab = jnp.concatenate([c, s_up, s_dn], axis=1)
    return jnp.concatenate([k_tab * (A_QK_DIM ** -0.5), k_tab], axis=1).astype(F32)


def _ret_rope_table(seq):
    pos = jnp.arange(seq, dtype=F32)
    inv = 1.0 / (RET_THETA ** jnp.linspace(0.0, 1.0, R_QK_DIM // 2, dtype=F32))
    ang = pos[:, None] * inv[None, :]
    cos, sin = jnp.cos(ang), jnp.sin(ang)
    half = R_QK_DIM // 2
    dd = jnp.arange(LANES) % R_QK_DIM
    j = dd % half
    c = cos[:, j]
    s_up = jnp.where(dd[None, :] < half, -sin[:, j], 0.0)
    s_dn = jnp.where(dd[None, :] >= half, sin[:, j], 0.0)
    scale = jnp.where(jnp.arange(LANES) < R_QK_DIM, 1.0, R_QK_DIM ** -0.5)[None, :]
    return jnp.concatenate([c * scale, s_up * scale, s_dn * scale], axis=1).astype(F32)


def _ret_decay_tables(chunk):
    log_g = jnp.log(1.0 - 2.0 ** (-5.0 - jnp.arange(R_HEADS, dtype=F32)))
    idx = jnp.arange(chunk, dtype=F32)
    diff = idx[:, None] - idx[None, :]
    dmask = jnp.where(diff >= 0,
                      jnp.exp(jnp.where(diff >= 0, diff, 0.0)[None] * log_g[:, None, None]),
                      0.0)
    ones = jnp.ones((1, 1, LANES), F32)
    xi = jnp.exp((idx + 1.0)[None, :] * log_g[:, None])[:, :, None] * ones
    zeta = jnp.exp((chunk - 1 - idx)[None, :] * log_g[:, None])[:, :, None] * ones
    dec = jnp.exp(chunk * log_g)[:, None, None] * ones
    return dmask.astype(F32), xi.astype(F32), zeta.astype(F32), dec.astype(F32)


def _split_w_in(w_in_l):
    d = w_in_l.shape[0]
    g = GROUP
    mix = w_in_l[:, :8 * g]
    rq = w_in_l[:, 8 * g:8 * g + g // 2].reshape(d, R_HEADS, R_QK_DIM)
    rk = w_in_l[:, 8 * g + g // 2:9 * g].reshape(d, R_HEADS, R_QK_DIM)
    rqk = jnp.concatenate([rq, rk], axis=2).reshape(d, g)
    rest = w_in_l[:, 9 * g:11 * g]
    w1 = jnp.concatenate([mix, rqk, rest], axis=1).astype(BF16)
    wg = w_in_l[:, 11 * g:].astype(BF16)
    return w1, wg


def kernel(x, norm_g, w_in, attn_lambda, attn_subln_g, conv_w, w_branch, w_out, final_norm_g):
    b, s, d = x.shape
    depth = w_in.shape[0]
    x2 = x.reshape(b * s, d)
    atab = _attn_rope_table(s)
    rtab = _ret_rope_table(s)
    dmask, xi, zeta, dec = _ret_decay_tables(RET_CHUNK)
    final_g = final_norm_g.reshape(1, d)
    for layer in range(depth):
        lam_init = 0.8 - 0.6 * math.exp(-0.3 * layer)
        w1, wg = _split_w_in(w_in[layer])
        g = norm_g[layer].reshape(1, d)
        aq, ak, av, az, c, rq, rk, rv, rz = _inproj(x2, g, w1, atab, rtab, conv_w[layer], s)
        sh = (b, s, GROUP)
        a = _attention(aq.reshape(sh), ak.reshape(sh), av.reshape(sh), az.reshape(sh),
                       attn_lambda[layer], attn_subln_g[layer].reshape(1, LANES), lam_init)
        r = _retention(rq.reshape(sh), rk.reshape(sh), rv.reshape(sh), rz.reshape(sh),
                       dmask, xi, zeta, dec)
        x2 = _merge(x2, a.reshape(b * s, GROUP), c, r.reshape(b * s, GROUP), g, wg,
                    w_branch[layer].astype(BF16), w_out[layer].astype(BF16), final_g,
                    layer == depth - 1)
    return x2.reshape(b, s, d)
```

```python
import functools
import math

import jax
import jax.numpy as jnp
from jax import lax
from jax.experimental import pallas as pl
from jax.experimental.pallas import tpu as pltpu

F32 = jnp.float32
BF16 = jnp.bfloat16

A_HEADS = 4
A_QK_DIM = 64
ROPE_THETA = 500000.0
ROPE_DIM = A_QK_DIM // 4
NEG_INF = -1e30
CONV_WIDTH = 3
R_HEADS = 4
R_QK_DIM = 64
RET_THETA = 10000.0
N_BRANCH = 3
EPS = 1e-6

LANES = 128
GROUP = 512
VMEM_LIMIT = 56 * 1024 * 1024

PROJ_ROWS = 512
ATT_TILE = 512
RET_CHUNK = 512


def _silu(t):
    return t * (1.0 / (1.0 + jnp.exp(-t)))


def _rms(x):
    return x * lax.rsqrt(jnp.mean(x * x, axis=-1, keepdims=True) + EPS)


def _rope_block(blk, c, s_up, s_dn, shift):
    return (blk * c + pltpu.roll(blk, LANES - shift, 1) * s_up
            + pltpu.roll(blk, shift, 1) * s_dn)


def _inproj_kernel(x_ref, g_ref, w_ref, atab_ref, rtab_ref, cw_ref,
                   aq_ref, ak_ref, av_ref, az_ref, c_ref,
                   rq_ref, rk_ref, rv_ref, rz_ref, ubuf_ref, *, tiles_per_seq):
    i = pl.program_id(0)
    tm = x_ref.shape[0]
    h = (_rms(x_ref[...]) * g_ref[...]).astype(BF16)

    def proj(gi):
        return jnp.dot(h, w_ref[:, gi * GROUP:(gi + 1) * GROUP], preferred_element_type=F32)

    for gi, out_ref, t0 in ((0, aq_ref, 0), (1, ak_ref, 3 * LANES)):
        y = proj(gi)
        c = atab_ref[:, t0:t0 + LANES]
        s_up = atab_ref[:, t0 + LANES:t0 + 2 * LANES]
        s_dn = atab_ref[:, t0 + 2 * LANES:t0 + 3 * LANES]
        for hh in range(A_HEADS):
            sl = slice(hh * LANES, (hh + 1) * LANES)
            out_ref[:, sl] = _rope_block(y[:, sl], c, s_up, s_dn, ROPE_DIM // 2).astype(BF16)
    av_ref[...] = proj(2).astype(BF16)
    az_ref[...] = _silu(proj(3)).astype(BF16)

    u = proj(4) * proj(6)

    @pl.when(i % tiles_per_seq == 0)
    def _():
        ubuf_ref[0:8, :] = jnp.zeros((8, GROUP), F32)

    ubuf_ref[8:8 + tm, :] = u
    conv = (cw_ref[0:1, :] * ubuf_ref[6:6 + tm, :] + cw_ref[1:2, :] * ubuf_ref[7:7 + tm, :]
            + cw_ref[2:3, :] * u)
    ubuf_ref[0:8, :] = u[tm - 8:tm, :]
    c_ref[...] = (proj(5) * conv * _silu(proj(7))).astype(BF16)

    y = proj(8)
    c = rtab_ref[:, 0:LANES]
    s_up = rtab_ref[:, LANES:2 * LANES]
    s_dn = rtab_ref[:, 2 * LANES:3 * LANES]
    lane = lax.broadcasted_iota(jnp.int32, (tm, LANES), 1)
    low = lane < R_QK_DIM
    for hh in range(R_HEADS):
        sl = slice(hh * LANES, (hh + 1) * LANES)
        rot = _rope_block(y[:, sl], c, s_up, s_dn, R_QK_DIM // 2)
        rq_ref[:, sl] = jnp.where(low, rot, 0.0).astype(BF16)
        rk_ref[:, sl] = jnp.where(low, pltpu.roll(rot, R_QK_DIM, 1), 0.0).astype(BF16)
    rv_ref[...] = proj(9).astype(BF16)
    rz_ref[...] = _silu(proj(10)).astype(BF16)


def _inproj(x2, norm_g, w1, atab, rtab, conv_w, seq):
    n, d = x2.shape
    tm = PROJ_ROWS
    tiles_per_seq = seq // tm
    row = lambda i: (i, 0)
    pos = lambda i: (i % tiles_per_seq, 0)
    const = lambda i: (0, 0)
    out_sds = jax.ShapeDtypeStruct((n, GROUP), BF16)
    out_spec = pl.BlockSpec((tm, GROUP), row)
    return pl.pallas_call(
        functools.partial(_inproj_kernel, tiles_per_seq=tiles_per_seq),
        grid=(n // tm,),
        in_specs=[
            pl.BlockSpec((tm, d), row),
            pl.BlockSpec((1, d), const),
            pl.BlockSpec(w1.shape, const),
            pl.BlockSpec((tm, atab.shape[1]), pos),
            pl.BlockSpec((tm, rtab.shape[1]), pos),
            pl.BlockSpec(conv_w.shape, const),
        ],
        out_specs=[out_spec] * 9,
        out_shape=[out_sds] * 9,
        scratch_shapes=[pltpu.VMEM((tm + 8, GROUP), F32)],
        compiler_params=pltpu.CompilerParams(
            dimension_semantics=("arbitrary",), vmem_limit_bytes=VMEM_LIMIT),
        name="inproj",
    )(x2, norm_g, w1, atab, rtab, conv_w)


def _attn_kernel(lam_ref, g_ref, q_ref, k_ref, v_ref, z_ref, o_ref,
                 m_ref, l_ref, acc_ref, *, lam_init):
    i = pl.program_id(2)
    tq = q_ref.shape[0]
    q = q_ref[...]
    lane = lax.broadcasted_iota(jnp.int32, q.shape, 1)
    zero = jnp.zeros_like(q)
    qs = jnp.concatenate([jnp.where(lane < A_QK_DIM, q, zero),
                          jnp.where(lane >= A_QK_DIM, q, zero)], axis=0)
    m_ref[...] = jnp.full(m_ref.shape, NEG_INF, F32)
    l_ref[...] = jnp.zeros(l_ref.shape, F32)
    acc_ref[...] = jnp.zeros(acc_ref.shape, F32)
    nchunk = tq // LANES

    def step(j, masked):
        off = pl.multiple_of(j * tq, tq)
        ks = k_ref[pl.ds(off, tq), :]
        vs = v_ref[pl.ds(off, tq), :]
        s = lax.dot_general(qs, ks, (((1,), (1,)), ((), ())), preferred_element_type=F32)
        if masked:
            r = lax.broadcasted_iota(jnp.int32, s.shape, 0)
            cidx = lax.broadcasted_iota(jnp.int32, s.shape, 1)
            s = jnp.where(cidx <= jnp.where(r >= tq, r - tq, r), s, NEG_INF)
        m_prev = m_ref[...]
        m_new = jnp.maximum(m_prev, jnp.max(s, axis=1, keepdims=True))
        alpha = jnp.exp(m_prev - m_new)
        m_ref[...] = m_new
        p = jnp.exp(s - pltpu.repeat(m_new, nchunk, axis=1))
        psum = p[:, 0:LANES]
        for c in range(1, nchunk):
            psum = psum + p[:, c * LANES:(c + 1) * LANES]
        l_ref[...] = alpha * l_ref[...] + psum
        acc_ref[...] = alpha * acc_ref[...] + jnp.dot(p.astype(BF16), vs,
                                                       preferred_element_type=F32)

    def body(j, carry):
        step(j, False)
        return carry

    lax.fori_loop(0, i, body, 0)
    step(i, True)

    o = acc_ref[...] / jnp.sum(l_ref[...], axis=1, keepdims=True)
    al = lam_ref[...]
    lam = (jnp.exp(jnp.sum(al[0:1] * al[1:2], axis=1, keepdims=True))
           - jnp.exp(jnp.sum(al[2:3] * al[3:4], axis=1, keepdims=True)) + lam_init)
    d = o[:tq] - lam * o[tq:]
    d = _rms(d) * g_ref[...] * (1.0 - lam_init)
    o_ref[...] = (d * z_ref[...].astype(F32)).astype(BF16)


def _attention(aq, ak, av, az, attn_lambda, subln_g, lam_init):
    b, s, _ = aq.shape
    tq = ATT_TILE
    tile = pl.BlockSpec((None, tq, LANES), lambda bi, hi, qi: (bi, qi, hi))
    full = pl.BlockSpec((None, s, LANES), lambda bi, hi, qi: (bi, 0, hi))
    const = lambda bi, hi, qi: (0, 0)
    return pl.pallas_call(
        functools.partial(_attn_kernel, lam_init=lam_init),
        grid=(b, A_HEADS, s // tq),
        in_specs=[
            pl.BlockSpec(attn_lambda.shape, const),
            pl.BlockSpec(subln_g.shape, const),
            tile, full, full, tile,
        ],
        out_specs=tile,
        out_shape=jax.ShapeDtypeStruct(aq.shape, BF16),
        scratch_shapes=[pltpu.VMEM((2 * tq, LANES), F32), pltpu.VMEM((2 * tq, LANES), F32),
                        pltpu.VMEM((2 * tq, LANES), F32)],
        compiler_params=pltpu.CompilerParams(
            dimension_semantics=("arbitrary", "arbitrary", "arbitrary"),
            vmem_limit_bytes=VMEM_LIMIT),
        name="diff_attention",
    )(attn_lambda, subln_g, aq, ak, av, az)


def _ret_kernel(q_ref, k_ref, v_ref, z_ref, dm_ref, xi_ref, zeta_ref, dec_ref, o_ref, st_ref):
    @pl.when(pl.program_id(1) == 0)
    def _():
        st_ref[...] = jnp.zeros(st_ref.shape, F32)

    for hh in range(R_HEADS):
        sl = slice(hh * LANES, (hh + 1) * LANES)
        q = q_ref[:, sl]
        k = k_ref[:, sl]
        v = v_ref[:, sl]
        s = lax.dot_general(q, k, (((1,), (1,)), ((), ())), preferred_element_type=F32)
        inner = jnp.dot((s * dm_ref[hh]).astype(BF16), v, preferred_element_type=F32)
        st = st_ref[hh]
        cross = jnp.dot(q, st.astype(BF16), preferred_element_type=F32) * xi_ref[hh]
        o = _rms(inner + cross)
        o_ref[:, sl] = (o * z_ref[:, sl].astype(F32)).astype(BF16)
        kz = (k.astype(F32) * zeta_ref[hh]).astype(BF16)
        kv = lax.dot_general(kz, v, (((0,), (0,)), ((), ())), preferred_element_type=F32)
        st_ref[hh] = dec_ref[hh] * st + kv


def _retention(rq, rk, rv, rz, dmask, xi, zeta, dec):
    b, s, w = rq.shape
    c = RET_CHUNK
    tile = pl.BlockSpec((None, c, w), lambda bi, ti: (bi, ti, 0))
    const3 = lambda bi, ti: (0, 0, 0)
    return pl.pallas_call(
        _ret_kernel,
        grid=(b, s // c),
        in_specs=[tile, tile, tile, tile,
                  pl.BlockSpec(dmask.shape, const3), pl.BlockSpec(xi.shape, const3),
                  pl.BlockSpec(zeta.shape, const3), pl.BlockSpec(dec.shape, const3)],
        out_specs=tile,
        out_shape=jax.ShapeDtypeStruct(rq.shape, BF16),
        scratch_shapes=[pltpu.VMEM((R_HEADS, LANES, LANES), F32)],
        compiler_params=pltpu.CompilerParams(
            dimension_semantics=("arbitrary", "arbitrary"), vmem_limit_bytes=VMEM_LIMIT),
        name="retention",
    )(rq, rk, rv, rz, dmask, xi, zeta, dec)


def _merge_kernel(x_ref, a_ref, c_ref, r_ref, g_ref, wg_ref, wb_ref, wo_ref, fg_ref, o_ref,
                  *, final_norm):
    x = x_ref[...]
    d = x.shape[1]
    h = (_rms(x) * g_ref[...]).astype(BF16)
    merged = None
    for bi, br_ref in enumerate((a_ref, c_ref, r_ref)):
        t = jnp.dot(h, wg_ref[:, bi * d:(bi + 1) * d], preferred_element_type=F32)
        gate = 1.0 / (1.0 + jnp.exp(-t))
        contrib = gate * jnp.dot(br_ref[...], wb_ref[bi], preferred_element_type=F32)
        merged = contrib if merged is None else merged + contrib
    y = x + jnp.dot(merged.astype(BF16), wo_ref[...], preferred_element_type=F32)
    if final_norm:
        y = _rms(y) * fg_ref[...]
    o_ref[...] = y


def _merge(x2, a, c, r, norm_g, wg, wb, wo, final_g, final_norm):
    n, d = x2.shape
    tm = PROJ_ROWS
    row = lambda i: (i, 0)
    const = lambda i: (0, 0)
    br = pl.BlockSpec((tm, GROUP), row)
    return pl.pallas_call(
        functools.partial(_merge_kernel, final_norm=final_norm),
        grid=(n // tm,),
        in_specs=[pl.BlockSpec((tm, d), row), br, br, br,
                  pl.BlockSpec((1, d), const),
                  pl.BlockSpec(wg.shape, const),
                  pl.BlockSpec(wb.shape, lambda i: (0, 0, 0)),
                  pl.BlockSpec(wo.shape, const),
                  pl.BlockSpec((1, d), const)],
        out_specs=pl.BlockSpec((tm, d), row),
        out_shape=jax.ShapeDtypeStruct((n, d), F32),
        compiler_params=pltpu.CompilerParams(
            dimension_semantics=("arbitrary",), vmem_limit_bytes=VMEM_LIMIT),
        name="merge_outproj",
    )(x2, a, c, r, norm_g, wg, wb, wo, final_g)


def _attn_rope_table(seq):
    pos = jnp.arange(seq, dtype=F32)
    inv = ROPE_THETA ** (-jnp.arange(0, ROPE_DIM, 2, dtype=F32) / ROPE_DIM)
    ang = pos[:, None] * inv[None, :]
    cos, sin = jnp.cos(ang), jnp.sin(ang)
    half = ROPE_DIM // 2
    dd = jnp.arange(LANES) % A_QK_DIM
    j = dd % half
    c = jnp.where(dd[None, :] < ROPE_DIM, cos[:, j], 1.0)
    s_up = jnp.where(dd[None, :] < half, -sin[:, j], 0.0)
    s_dn = jnp.where((dd[None, :] >= half) & (dd[None, :] < ROPE_DIM), sin[:, j], 0.0)
    k_tab = jnp.concatenate([c, s_up, s_dn], axis=1)
    return jnp.concatenate([k_tab * (A_QK_DIM ** -0.5), k_tab], axis=1).astype(F32)


def _ret_rope_table(seq):
    pos = jnp.arange(seq, dtype=F32)
    inv = 1.0 / (RET_THETA ** jnp.linspace(0.0, 1.0, R_QK_DIM // 2, dtype=F32))
    ang = pos[:, None] * inv[None, :]
    cos, sin = jnp.cos(ang), jnp.sin(ang)
    half = R_QK_DIM // 2
    dd = jnp.arange(LANES) % R_QK_DIM
    j = dd % half
    c = cos[:, j]
    s_up = jnp.where(dd[None, :] < half, -sin[:, j], 0.0)
    s_dn = jnp.where(dd[None, :] >= half, sin[:, j], 0.0)
    scale = jnp.where(jnp.arange(LANES) < R_QK_DIM, 1.0, R_QK_DIM ** -0.5)[None, :]
    return jnp.concatenate([c * scale, s_up * scale, s_dn * scale], axis=1).astype(F32)


def _ret_decay_tables(chunk):
    log_g = jnp.log(1.0 - 2.0 ** (-5.0 - jnp.arange(R_HEADS, dtype=F32)))
    idx = jnp.arange(chunk, dtype=F32)
    diff = idx[:, None] - idx[None, :]
    dmask = jnp.where(diff >= 0,
                      jnp.exp(jnp.where(diff >= 0, diff, 0.0)[None] * log_g[:, None, None]),
                      0.0)
    ones = jnp.ones((1, 1, LANES), F32)
    xi = jnp.exp((idx + 1.0)[None, :] * log_g[:, None])[:, :, None] * ones
    zeta = jnp.exp((chunk - 1 - idx)[None, :] * log_g[:, None])[:, :, None] * ones
    dec = jnp.exp(chunk * log_g)[:, None, None] * ones
    return dmask.astype(F32), xi.astype(F32), zeta.astype(F32), dec.astype(F32)


def _split_w_in(w_in_l):
    d = w_in_l.shape[0]
    g = GROUP
    mix = w_in_l[:, :8 * g]
    rq = w_in_l[:, 8 * g:8 * g + g // 2].reshape(d, R_HEADS, R_QK_DIM)
    rk = w_in_l[:, 8 * g + g // 2:9 * g].reshape(d, R_HEADS, R_QK_DIM)
    rqk = jnp.concatenate([rq, rk], axis=2).reshape(d, g)
    rest = w_in_l[:, 9 * g:11 * g]
    w1 = jnp.concatenate([mix, rqk, rest], axis=1).astype(BF16)
    wg = w_in_l[:, 11 * g:].astype(BF16)
    return w1, wg


def kernel(x, norm_g, w_in, attn_lambda, attn_subln_g, conv_w, w_branch, w_out, final_norm_g):
    b, s, d = x.shape
    depth = w_in.shape[0]
    x2 = x.reshape(b * s, d)
    atab = _attn_rope_table(s)
    rtab = _ret_rope_table(s)
    dmask, xi, zeta, dec = _ret_decay_tables(RET_CHUNK)
    final_g = final_norm_g.reshape(1, d)
    for layer in range(depth):
        lam_init = 0.8 - 0.6 * math.exp(-0.3 * layer)
        w1, wg = _split_w_in(w_in[layer])
        g = norm_g[layer].reshape(1, d)
        aq, ak, av, az, c, rq, rk, rv, rz = _inproj(x2, g, w1, atab, rtab, conv_w[layer], s)
        sh = (b, s, GROUP)
        a = _attention(aq.reshape(sh), ak.reshape(sh), av.reshape(sh), az.reshape(sh),
                       attn_lambda[layer], attn_subln_g[layer].reshape(1, LANES), lam_init)
        r = _retention(rq.reshape(sh), rk.reshape(sh), rv.reshape(sh), rz.reshape(sh),
                       dmask, xi, zeta, dec)
        x2 = _merge(x2, a.reshape(b * s, GROUP), c, r.reshape(b * s, GROUP), g, wg,
                    w_branch[layer].astype(BF16), w_out[layer].astype(BF16), final_g,
                    layer == depth - 1)
    return x2.reshape(b, s, d)
```

```python
import functools
import math

import jax
import jax.numpy as jnp
from jax import lax
from jax.experimental import pallas as pl
from jax.experimental.pallas import tpu as pltpu

F32 = jnp.float32
BF16 = jnp.bfloat16

A_HEADS = 4
A_QK_DIM = 64
ROPE_THETA = 500000.0
ROPE_DIM = A_QK_DIM // 4
NEG_INF = -1e30
CONV_WIDTH = 3
R_HEADS = 4
R_QK_DIM = 64
RET_THETA = 10000.0
N_BRANCH = 3
EPS = 1e-6

LANES = 128
GROUP = 512
VMEM_LIMIT = 56 * 1024 * 1024

PROJ_ROWS = 512
ATT_TILE = 512
ROW_BLOCK = 64
RET_CHUNK = 512


def _silu(t):
    return t * (1.0 / (1.0 + jnp.exp(-t)))


def _rms(x):
    return x * lax.rsqrt(jnp.mean(x * x, axis=-1, keepdims=True) + EPS)


def _rope_block(blk, c, s_up, s_dn, shift):
    return (blk * c + pltpu.roll(blk, LANES - shift, 1) * s_up
            + pltpu.roll(blk, shift, 1) * s_dn)


def _inproj_kernel(x_ref, g_ref, w_ref, atab_ref, rtab_ref, cw_ref,
                   aq_ref, ak_ref, av_ref, az_ref, c_ref,
                   rq_ref, rk_ref, rv_ref, rz_ref, ubuf_ref, *, tiles_per_seq):
    i = pl.program_id(0)
    tm = x_ref.shape[0]
    h = (_rms(x_ref[...]) * g_ref[...]).astype(BF16)

    def proj(gi):
        return jnp.dot(h, w_ref[:, gi * GROUP:(gi + 1) * GROUP], preferred_element_type=F32)

    for gi, out_ref, t0 in ((0, aq_ref, 0), (1, ak_ref, 3 * LANES)):
        y = proj(gi)
        c = atab_ref[:, t0:t0 + LANES]
        s_up = atab_ref[:, t0 + LANES:t0 + 2 * LANES]
        s_dn = atab_ref[:, t0 + 2 * LANES:t0 + 3 * LANES]
        for hh in range(A_HEADS):
            sl = slice(hh * LANES, (hh + 1) * LANES)
            out_ref[:, sl] = _rope_block(y[:, sl], c, s_up, s_dn, ROPE_DIM // 2).astype(BF16)
    av_ref[...] = proj(2).astype(BF16)
    az_ref[...] = _silu(proj(3)).astype(BF16)

    u = proj(4) * proj(6)

    @pl.when(i % tiles_per_seq == 0)
    def _():
        ubuf_ref[0:8, :] = jnp.zeros((8, GROUP), F32)

    ubuf_ref[8:8 + tm, :] = u
    conv = (cw_ref[0:1, :] * ubuf_ref[6:6 + tm, :] + cw_ref[1:2, :] * ubuf_ref[7:7 + tm, :]
            + cw_ref[2:3, :] * u)
    ubuf_ref[0:8, :] = u[tm - 8:tm, :]
    c_ref[...] = (proj(5) * conv * _silu(proj(7))).astype(BF16)

    y = proj(8)
    c = rtab_ref[:, 0:LANES]
    s_up = rtab_ref[:, LANES:2 * LANES]
    s_dn = rtab_ref[:, 2 * LANES:3 * LANES]
    lane = lax.broadcasted_iota(jnp.int32, (tm, LANES), 1)
    low = lane < R_QK_DIM
    for hh in range(R_HEADS):
        sl = slice(hh * LANES, (hh + 1) * LANES)
        rot = _rope_block(y[:, sl], c, s_up, s_dn, R_QK_DIM // 2)
        rq_ref[:, sl] = jnp.where(low, rot, 0.0).astype(BF16)
        rk_ref[:, sl] = jnp.where(low, pltpu.roll(rot, R_QK_DIM, 1), 0.0).astype(BF16)
    rv_ref[...] = proj(9).astype(BF16)
    rz_ref[...] = _silu(proj(10)).astype(BF16)


def _inproj(x2, norm_g, w1, atab, rtab, conv_w, seq):
    n, d = x2.shape
    tm = PROJ_ROWS
    tiles_per_seq = seq // tm
    row = lambda i: (i, 0)
    pos = lambda i: (i % tiles_per_seq, 0)
    const = lambda i: (0, 0)
    out_sds = jax.ShapeDtypeStruct((n, GROUP), BF16)
    out_spec = pl.BlockSpec((tm, GROUP), row)
    return pl.pallas_call(
        functools.partial(_inproj_kernel, tiles_per_seq=tiles_per_seq),
        grid=(n // tm,),
        in_specs=[
            pl.BlockSpec((tm, d), row),
            pl.BlockSpec((1, d), const),
            pl.BlockSpec(w1.shape, const),
            pl.BlockSpec((tm, atab.shape[1]), pos),
            pl.BlockSpec((tm, rtab.shape[1]), pos),
            pl.BlockSpec(conv_w.shape, const),
        ],
        out_specs=[out_spec] * 9,
        out_shape=[out_sds] * 9,
        scratch_shapes=[pltpu.VMEM((tm + 8, GROUP), F32)],
        compiler_params=pltpu.CompilerParams(
            dimension_semantics=("arbitrary",), vmem_limit_bytes=VMEM_LIMIT),
        name="inproj",
    )(x2, norm_g, w1, atab, rtab, conv_w)


def _attn_kernel(lam_ref, g_ref, q_ref, k_ref, v_ref, z_ref, o_ref,
                 qs_ref, m_ref, l_ref, acc_ref, s0_ref, s1_ref, p0_ref, p1_ref,
                 a0_ref, a1_ref, *, lam_init):
    s_refs, p_refs, alpha_refs = (s0_ref, s1_ref), (p0_ref, p1_ref), (a0_ref, a1_ref)
    i = pl.program_id(2)
    tq = q_ref.shape[0]
    nchunk = tq // LANES
    nblock = 2 * tq // ROW_BLOCK
    q = q_ref[...]
    lane = lax.broadcasted_iota(jnp.int32, q.shape, 1)
    zero = jnp.zeros_like(q)
    qs_ref[0:tq, :] = jnp.where(lane < A_QK_DIM, q, zero)
    qs_ref[tq:2 * tq, :] = jnp.where(lane >= A_QK_DIM, q, zero)
    m_ref[...] = jnp.full(m_ref.shape, NEG_INF, F32)
    l_ref[...] = jnp.zeros(l_ref.shape, F32)
    acc_ref[...] = jnp.zeros(acc_ref.shape, F32)

    def tile_rows(n):
        tile = i if isinstance(n, int) and n == 0 else jnp.where(n == 0, i, n - 1)
        return pl.ds(pl.multiple_of(tile * tq, tq), tq)

    def produce(n, slot, masked=False):
        s = lax.dot_general(qs_ref[...], k_ref[tile_rows(n), :], (((1,), (1,)), ((), ())),
                            preferred_element_type=F32)
        if masked:
            r = lax.broadcasted_iota(jnp.int32, s.shape, 0)
            cidx = lax.broadcasted_iota(jnp.int32, s.shape, 1)
            s = jnp.where(cidx <= jnp.where(r >= tq, r - tq, r), s, NEG_INF)
        s_refs[slot][...] = s

    def softmax(slot):
        for rb in range(nblock):
            rows = slice(rb * ROW_BLOCK, (rb + 1) * ROW_BLOCK)
            m_prev = m_ref[rows, :]
            m_new = jnp.maximum(m_prev, jnp.max(s_refs[slot][rows, :], axis=1, keepdims=True))
            alpha = jnp.exp2(m_prev - m_new)
            m_ref[rows, :] = m_new
            alpha_refs[slot][rows, :] = alpha
            for c in range(nchunk):
                cols = slice(c * LANES, (c + 1) * LANES)
                p_refs[slot][rows, cols] = jnp.exp2(
                    (s_refs[slot][rows, cols] - m_new).astype(BF16))

    def pv(n, slot):
        vs = v_ref[tile_rows(n), :]
        res = jnp.dot(p_refs[slot][...], jnp.concatenate([vs, jnp.ones_like(vs)], axis=1),
                      preferred_element_type=F32)
        alpha = alpha_refs[slot][...]
        acc_ref[...] = alpha * acc_ref[...] + res[:, :LANES]
        l_ref[...] = alpha * l_ref[...] + res[:, LANES:]

    npos = i + 1
    produce(0, 0, masked=True)

    def pair(kk, carry):
        n = 2 * kk
        softmax(0)
        pv(n, 0)
        produce(n + 1, 1)
        softmax(1)
        pv(n + 1, 1)
        produce(n + 2, 0)
        return carry

    npair = (npos - 1) // 2
    lax.fori_loop(0, npair, pair, 0)
    done = 2 * npair

    @pl.when(npos - done == 1)
    def _():
        softmax(0)
        pv(done, 0)

    @pl.when(npos - done == 2)
    def _():
        produce(done + 1, 1)
        softmax(0)
        pv(done, 0)
        softmax(1)
        pv(done + 1, 1)

    o = acc_ref[...] / l_ref[...]
    al = lam_ref[...]
    lam = (jnp.exp(jnp.sum(al[0:1] * al[1:2], axis=1, keepdims=True))
           - jnp.exp(jnp.sum(al[2:3] * al[3:4], axis=1, keepdims=True)) + lam_init)
    d = o[:tq] - lam * o[tq:]
    d = _rms(d) * g_ref[...] * (1.0 - lam_init)
    o_ref[...] = (d * z_ref[...].astype(F32)).astype(BF16)


def _attention(aq, ak, av, az, attn_lambda, subln_g, lam_init):
    b, s, _ = aq.shape
    tq = ATT_TILE
    tile = pl.BlockSpec((None, tq, LANES), lambda bi, hi, qi: (bi, qi, hi))
    full = pl.BlockSpec((None, s, LANES), lambda bi, hi, qi: (bi, 0, hi))
    const = lambda bi, hi, qi: (0, 0)
    return pl.pallas_call(
        functools.partial(_attn_kernel, lam_init=lam_init),
        grid=(b, A_HEADS, s // tq),
        in_specs=[
            pl.BlockSpec(attn_lambda.shape, const),
            pl.BlockSpec(subln_g.shape, const),
            tile, full, full, tile,
        ],
        out_specs=tile,
        out_shape=jax.ShapeDtypeStruct(aq.shape, BF16),
        scratch_shapes=[pltpu.VMEM((2 * tq, LANES), BF16),
                        pltpu.VMEM((2 * tq, LANES), F32), pltpu.VMEM((2 * tq, LANES), F32),
                        pltpu.VMEM((2 * tq, LANES), F32),
                        pltpu.VMEM((2 * tq, tq), F32), pltpu.VMEM((2 * tq, tq), F32),
                        pltpu.VMEM((2 * tq, tq), BF16), pltpu.VMEM((2 * tq, tq), BF16),
                        pltpu.VMEM((2 * tq, LANES), F32), pltpu.VMEM((2 * tq, LANES), F32)],
        compiler_params=pltpu.CompilerParams(
            dimension_semantics=("arbitrary", "arbitrary", "arbitrary"),
            vmem_limit_bytes=VMEM_LIMIT),
        name="diff_attention",
    )(attn_lambda, subln_g, aq, ak, av, az)


def _ret_kernel(q_ref, k_ref, v_ref, z_ref, dm_ref, xi_ref, zeta_ref, dec_ref, o_ref, st_ref):
    @pl.when(pl.program_id(1) == 0)
    def _():
        st_ref[...] = jnp.zeros(st_ref.shape, F32)

    for hh in range(R_HEADS):
        sl = slice(hh * LANES, (hh + 1) * LANES)
        q = q_ref[:, sl]
        k = k_ref[:, sl]
        v = v_ref[:, sl]
        s = lax.dot_general(q, k, (((1,), (1,)), ((), ())), preferred_element_type=F32)
        inner = jnp.dot((s * dm_ref[hh]).astype(BF16), v, preferred_element_type=F32)
        st = st_ref[hh]
        cross = jnp.dot(q, st.astype(BF16), preferred_element_type=F32) * xi_ref[hh]
        o = _rms(inner + cross)
        o_ref[:, sl] = (o * z_ref[:, sl].astype(F32)).astype(BF16)
        kz = (k.astype(F32) * zeta_ref[hh]).astype(BF16)
        kv = lax.dot_general(kz, v, (((0,), (0,)), ((), ())), preferred_element_type=F32)
        st_ref[hh] = dec_ref[hh] * st + kv


def _retention(rq, rk, rv, rz, dmask, xi, zeta, dec):
    b, s, w = rq.shape
    c = RET_CHUNK
    tile = pl.BlockSpec((None, c, w), lambda bi, ti: (bi, ti, 0))
    const3 = lambda bi, ti: (0, 0, 0)
    return pl.pallas_call(
        _ret_kernel,
        grid=(b, s // c),
        in_specs=[tile, tile, tile, tile,
                  pl.BlockSpec(dmask.shape, const3), pl.BlockSpec(xi.shape, const3),
                  pl.BlockSpec(zeta.shape, const3), pl.BlockSpec(dec.shape, const3)],
        out_specs=tile,
        out_shape=jax.ShapeDtypeStruct(rq.shape, BF16),
        scratch_shapes=[pltpu.VMEM((R_HEADS, LANES, LANES), F32)],
        compiler_params=pltpu.CompilerParams(
            dimension_semantics=("arbitrary", "arbitrary"), vmem_limit_bytes=VMEM_LIMIT),
        name="retention",
    )(rq, rk, rv, rz, dmask, xi, zeta, dec)


def _merge_kernel(x_ref, a_ref, c_ref, r_ref, g_ref, wg_ref, wb_ref, wo_ref, fg_ref, o_ref,
                  *, final_norm):
    x = x_ref[...]
    d = x.shape[1]
    h = (_rms(x) * g_ref[...]).astype(BF16)
    merged = None
    for bi, br_ref in enumerate((a_ref, c_ref, r_ref)):
        t = jnp.dot(h, wg_ref[:, bi * d:(bi + 1) * d], preferred_element_type=F32)
        gate = 1.0 / (1.0 + jnp.exp(-t))
        contrib = gate * jnp.dot(br_ref[...], wb_ref[bi], preferred_element_type=F32)
        merged = contrib if merged is None else merged + contrib
    y = x + jnp.dot(merged.astype(BF16), wo_ref[...], preferred_element_type=F32)
    if final_norm:
        y = _rms(y) * fg_ref[...]
    o_ref[...] = y


def _merge(x2, a, c, r, norm_g, wg, wb, wo, final_g, final_norm):
    n, d = x2.shape
    tm = PROJ_ROWS
    row = lambda i: (i, 0)
    const = lambda i: (0, 0)
    br = pl.BlockSpec((tm, GROUP), row)
    return pl.pallas_call(
        functools.partial(_merge_kernel, final_norm=final_norm),
        grid=(n // tm,),
        in_specs=[pl.BlockSpec((tm, d), row), br, br, br,
                  pl.BlockSpec((1, d), const),
                  pl.BlockSpec(wg.shape, const),
                  pl.BlockSpec(wb.shape, lambda i: (0, 0, 0)),
                  pl.BlockSpec(wo.shape, const),
                  pl.BlockSpec((1, d), const)],
        out_specs=pl.BlockSpec((tm, d), row),
        out_shape=jax.ShapeDtypeStruct((n, d), F32),
        compiler_params=pltpu.CompilerParams(
            dimension_semantics=("arbitrary",), vmem_limit_bytes=VMEM_LIMIT),
        name="merge_outproj",
    )(x2, a, c, r, norm_g, wg, wb, wo, final_g)


def _attn_rope_table(seq):
    pos = jnp.arange(seq, dtype=F32)
    inv = ROPE_THETA ** (-jnp.arange(0, ROPE_DIM, 2, dtype=F32) / ROPE_DIM)
    ang = pos[:, None] * inv[None, :]
    cos, sin = jnp.cos(ang), jnp.sin(ang)
    half = ROPE_DIM // 2
    dd = jnp.arange(LANES) % A_QK_DIM
    j = dd % half
    c = jnp.where(dd[None, :] < ROPE_DIM, cos[:, j], 1.0)
    s_up = jnp.where(dd[None, :] < half, -sin[:, j], 0.0)
    s_dn = jnp.where((dd[None, :] >= half) & (dd[None, :] < ROPE_DIM), sin[:, j], 0.0)
    k_tab = jnp.concatenate([c, s_up, s_dn], axis=1)
    q_scale = (A_QK_DIM ** -0.5) * math.log2(math.e)
    return jnp.concatenate([k_tab * q_scale, k_tab], axis=1).astype(F32)


def _ret_rope_table(seq):
    pos = jnp.arange(seq, dtype=F32)
    inv = 1.0 / (RET_THETA ** jnp.linspace(0.0, 1.0, R_QK_DIM // 2, dtype=F32))
    ang = pos[:, None] * inv[None, :]
    cos, sin = jnp.cos(ang), jnp.sin(ang)
    half = R_QK_DIM // 2
    dd = jnp.arange(LANES) % R_QK_DIM
    j = dd % half
    c = cos[:, j]
    s_up = jnp.where(dd[None, :] < half, -sin[:, j], 0.0)
    s_dn = jnp.where(dd[None, :] >= half, sin[:, j], 0.0)
    scale = jnp.where(jnp.arange(LANES) < R_QK_DIM, 1.0, R_QK_DIM ** -0.5)[None, :]
    return jnp.concatenate([c * scale, s_up * scale, s_dn * scale], axis=1).astype(F32)


def _ret_decay_tables(chunk):
    log_g = jnp.log(1.0 - 2.0 ** (-5.0 - jnp.arange(R_HEADS, dtype=F32)))
    idx = jnp.arange(chunk, dtype=F32)
    diff = idx[:, None] - idx[None, :]
    dmask = jnp.where(diff >= 0,
                      jnp.exp(jnp.where(diff >= 0, diff, 0.0)[None] * log_g[:, None, None]),
                      0.0)
    ones = jnp.ones((1, 1, LANES), F32)
    xi = jnp.exp((idx + 1.0)[None, :] * log_g[:, None])[:, :, None] * ones
    zeta = jnp.exp((chunk - 1 - idx)[None, :] * log_g[:, None])[:, :, None] * ones
    dec = jnp.exp(chunk * log_g)[:, None, None] * ones
    return dmask.astype(F32), xi.astype(F32), zeta.astype(F32), dec.astype(F32)


def _split_w_in(w_in_l):
    d = w_in_l.shape[0]
    g = GROUP
    mix = w_in_l[:, :8 * g]
    rq = w_in_l[:, 8 * g:8 * g + g // 2].reshape(d, R_HEADS, R_QK_DIM)
    rk = w_in_l[:, 8 * g + g // 2:9 * g].reshape(d, R_HEADS, R_QK_DIM)
    rqk = jnp.concatenate([rq, rk], axis=2).reshape(d, g)
    rest = w_in_l[:, 9 * g:11 * g]
    w1 = jnp.concatenate([mix, rqk, rest], axis=1).astype(BF16)
    wg = w_in_l[:, 11 * g:].astype(BF16)
    return w1, wg


def kernel(x, norm_g, w_in, attn_lambda, attn_subln_g, conv_w, w_branch, w_out, final_norm_g):
    b, s, d = x.shape
    depth = w_in.shape[0]
    x2 = x.reshape(b * s, d)
    atab = _attn_rope_table(s)
    rtab = _ret_rope_table(s)
    dmask, xi, zeta, dec = _ret_decay_tables(RET_CHUNK)
    final_g = final_norm_g.reshape(1, d)
    for layer in range(depth):
        lam_init = 0.8 - 0.6 * math.exp(-0.3 * layer)
        w1, wg = _split_w_in(w_in[layer])
        g = norm_g[layer].reshape(1, d)
        aq, ak, av, az, c, rq, rk, rv, rz = _inproj(x2, g, w1, atab, rtab, conv_w[layer], s)
        sh = (b, s, GROUP)
        a = _attention(aq.reshape(sh), ak.reshape(sh), av.reshape(sh), az.reshape(sh),
                       attn_lambda[layer], attn_subln_g[layer].reshape(1, LANES), lam_init)
        r = _retention(rq.reshape(sh), rk.reshape(sh), rv.reshape(sh), rz.reshape(sh),
                       dmask, xi, zeta, dec)
        x2 = _merge(x2, a.reshape(b * s, GROUP), c, r.reshape(b * s, GROUP), g, wg,
                    w_branch[layer].astype(BF16), w_out[layer].astype(BF16), final_g,
                    layer == depth - 1)
    return x2.reshape(b, s, d)
```

```python
import functools
import math

import jax
import jax.numpy as jnp
from jax import lax
from jax.experimental import pallas as pl
from jax.experimental.pallas import tpu as pltpu

F32 = jnp.float32
BF16 = jnp.bfloat16

A_HEADS = 4
A_QK_DIM = 64
ROPE_THETA = 500000.0
ROPE_DIM = A_QK_DIM // 4
NEG_INF = -1e30
CONV_WIDTH = 3
R_HEADS = 4
R_QK_DIM = 64
RET_THETA = 10000.0
N_BRANCH = 3
EPS = 1e-6

LANES = 128
GROUP = 512
VMEM_LIMIT = 56 * 1024 * 1024

PROJ_ROWS = 512
ATT_TILE = 512
ROW_BLOCK = 64
RET_CHUNK = 512


def _silu(t):
    return t * (1.0 / (1.0 + jnp.exp(-t)))


def _rms(x):
    return x * lax.rsqrt(jnp.mean(x * x, axis=-1, keepdims=True) + EPS)


def _rope_block(blk, c, s_up, s_dn, shift):
    return (blk * c + pltpu.roll(blk, LANES - shift, 1) * s_up
            + pltpu.roll(blk, shift, 1) * s_dn)


def _inproj_kernel(x_ref, g_ref, w_ref, atab_ref, rtab_ref, cw_ref,
                   aq_ref, ak_ref, av_ref, az_ref, c_ref,
                   rq_ref, rk_ref, rv_ref, rz_ref, ubuf_ref, *, tiles_per_seq):
    i = pl.program_id(0)
    tm = x_ref.shape[0]
    h = (_rms(x_ref[...]) * g_ref[...]).astype(BF16)

    def proj(gi):
        return jnp.dot(h, w_ref[:, gi * GROUP:(gi + 1) * GROUP], preferred_element_type=F32)

    for gi, out_ref, t0 in ((0, aq_ref, 0), (1, ak_ref, 3 * LANES)):
        y = proj(gi)
        c = atab_ref[:, t0:t0 + LANES]
        s_up = atab_ref[:, t0 + LANES:t0 + 2 * LANES]
        s_dn = atab_ref[:, t0 + 2 * LANES:t0 + 3 * LANES]
        for hh in range(A_HEADS):
            sl = slice(hh * LANES, (hh + 1) * LANES)
            out_ref[:, sl] = _rope_block(y[:, sl], c, s_up, s_dn, ROPE_DIM // 2).astype(BF16)
    av_ref[...] = proj(2).astype(BF16)
    az_ref[...] = _silu(proj(3)).astype(BF16)

    u = proj(4) * proj(6)

    @pl.when(i % tiles_per_seq == 0)
    def _():
        ubuf_ref[0:8, :] = jnp.zeros((8, GROUP), F32)

    ubuf_ref[8:8 + tm, :] = u
    conv = (cw_ref[0:1, :] * ubuf_ref[6:6 + tm, :] + cw_ref[1:2, :] * ubuf_ref[7:7 + tm, :]
            + cw_ref[2:3, :] * u)
    ubuf_ref[0:8, :] = u[tm - 8:tm, :]
    c_ref[...] = (proj(5) * conv * _silu(proj(7))).astype(BF16)

    y = proj(8)
    c = rtab_ref[:, 0:LANES]
    s_up = rtab_ref[:, LANES:2 * LANES]
    s_dn = rtab_ref[:, 2 * LANES:3 * LANES]
    lane = lax.broadcasted_iota(jnp.int32, (tm, LANES), 1)
    low = lane < R_QK_DIM
    for hh in range(R_HEADS):
        sl = slice(hh * LANES, (hh + 1) * LANES)
        rot = _rope_block(y[:, sl], c, s_up, s_dn, R_QK_DIM // 2)
        rq_ref[:, sl] = jnp.where(low, rot, 0.0).astype(BF16)
        rk_ref[:, sl] = jnp.where(low, pltpu.roll(rot, R_QK_DIM, 1), 0.0).astype(BF16)
    rv_ref[...] = proj(9).astype(BF16)
    rz_ref[...] = _silu(proj(10)).astype(BF16)


def _inproj(x2, norm_g, w1, atab, rtab, conv_w, seq):
    n, d = x2.shape
    tm = PROJ_ROWS
    tiles_per_seq = seq // tm
    row = lambda i: (i, 0)
    pos = lambda i: (i % tiles_per_seq, 0)
    const = lambda i: (0, 0)
    out_sds = jax.ShapeDtypeStruct((n, GROUP), BF16)
    out_spec = pl.BlockSpec((tm, GROUP), row)
    return pl.pallas_call(
        functools.partial(_inproj_kernel, tiles_per_seq=tiles_per_seq),
        grid=(n // tm,),
        in_specs=[
            pl.BlockSpec((tm, d), row),
            pl.BlockSpec((1, d), const),
            pl.BlockSpec(w1.shape, const),
            pl.BlockSpec((tm, atab.shape[1]), pos),
            pl.BlockSpec((tm, rtab.shape[1]), pos),
            pl.BlockSpec(conv_w.shape, const),
        ],
        out_specs=[out_spec] * 9,
        out_shape=[out_sds] * 9,
        scratch_shapes=[pltpu.VMEM((tm + 8, GROUP), F32)],
        compiler_params=pltpu.CompilerParams(
            dimension_semantics=("arbitrary",), vmem_limit_bytes=VMEM_LIMIT),
        name="inproj",
    )(x2, norm_g, w1, atab, rtab, conv_w)


def _attn_kernel(lam_ref, g_ref, q_ref, k_ref, v_ref, z_ref, o_ref,
                 qs_ref, m_ref, l_ref, acc_ref, s0_ref, s1_ref, p0_ref, p1_ref,
                 a0_ref, a1_ref, *, lam_init):
    s_refs, p_refs, alpha_refs = (s0_ref, s1_ref), (p0_ref, p1_ref), (a0_ref, a1_ref)
    i = pl.program_id(2)
    tq = q_ref.shape[0]
    nchunk = tq // LANES
    nblock = 2 * tq // ROW_BLOCK
    q = q_ref[...]
    lane = lax.broadcasted_iota(jnp.int32, q.shape, 1)
    zero = jnp.zeros_like(q)
    qs_ref[0:tq, :] = jnp.where(lane < A_QK_DIM, q, zero)
    qs_ref[tq:2 * tq, :] = jnp.where(lane >= A_QK_DIM, q, zero)
    m_ref[...] = jnp.full(m_ref.shape, NEG_INF, F32)
    l_ref[...] = jnp.zeros(l_ref.shape, F32)
    acc_ref[...] = jnp.zeros(acc_ref.shape, F32)

    def tile_rows(n):
        tile = i if isinstance(n, int) and n == 0 else jnp.where(n == 0, i, n - 1)
        return pl.ds(pl.multiple_of(tile * tq, tq), tq)

    def produce(n, slot, masked=False):
        s = lax.dot_general(qs_ref[...], k_ref[tile_rows(n), :], (((1,), (1,)), ((), ())),
                            preferred_element_type=F32)
        if masked:
            r = lax.broadcasted_iota(jnp.int32, s.shape, 0)
            cidx = lax.broadcasted_iota(jnp.int32, s.shape, 1)
            s = jnp.where(cidx <= jnp.where(r >= tq, r - tq, r), s, NEG_INF)
        s_refs[slot][...] = s

    def softmax(slot):
        for rb in range(nblock):
            rows = slice(rb * ROW_BLOCK, (rb + 1) * ROW_BLOCK)
            m_prev = m_ref[rows, :]
            m_new = jnp.maximum(m_prev, jnp.max(s_refs[slot][rows, :], axis=1, keepdims=True))
            alpha = jnp.exp2(m_prev - m_new)
            m_ref[rows, :] = m_new
            alpha_refs[slot][rows, :] = alpha
            for c in range(nchunk):
                cols = slice(c * LANES, (c + 1) * LANES)
                p_refs[slot][rows, cols] = jnp.exp2(
                    s_refs[slot][rows, cols] - m_new).astype(BF16)

    def pv(n, slot):
        vs = v_ref[tile_rows(n), :]
        res = jnp.dot(p_refs[slot][...], jnp.concatenate([vs, jnp.ones_like(vs)], axis=1),
                      preferred_element_type=F32)
        alpha = alpha_refs[slot][...]
        acc_ref[...] = alpha * acc_ref[...] + res[:, :LANES]
        l_ref[...] = alpha * l_ref[...] + res[:, LANES:]

    npos = i + 1
    produce(0, 0, masked=True)

    def pair(kk, carry):
        n = 2 * kk
        softmax(0)
        pv(n, 0)
        produce(n + 1, 1)
        softmax(1)
        pv(n + 1, 1)
        produce(n + 2, 0)
        return carry

    npair = (npos - 1) // 2
    lax.fori_loop(0, npair, pair, 0)
    done = 2 * npair

    @pl.when(npos - done == 1)
    def _():
        softmax(0)
        pv(done, 0)

    @pl.when(npos - done == 2)
    def _():
        produce(done + 1, 1)
        softmax(0)
        pv(done, 0)
        softmax(1)
        pv(done + 1, 1)

    o = acc_ref[...] / l_ref[...]
    al = lam_ref[...]
    lam = (jnp.exp(jnp.sum(al[0:1] * al[1:2], axis=1, keepdims=True))
           - jnp.exp(jnp.sum(al[2:3] * al[3:4], axis=1, keepdims=True)) + lam_init)
    d = o[:tq] - lam * o[tq:]
    d = _rms(d) * g_ref[...] * (1.0 - lam_init)
    o_ref[...] = (d * z_ref[...].astype(F32)).astype(BF16)


def _attention(aq, ak, av, az, attn_lambda, subln_g, lam_init):
    b, s, _ = aq.shape
    tq = ATT_TILE
    tile = pl.BlockSpec((None, tq, LANES), lambda bi, hi, qi: (bi, qi, hi))
    full = pl.BlockSpec((None, s, LANES), lambda bi, hi, qi: (bi, 0, hi))
    const = lambda bi, hi, qi: (0, 0)
    return pl.pallas_call(
        functools.partial(_attn_kernel, lam_init=lam_init),
        grid=(b, A_HEADS, s // tq),
        in_specs=[
            pl.BlockSpec(attn_lambda.shape, const),
            pl.BlockSpec(subln_g.shape, const),
            tile, full, full, tile,
        ],
        out_specs=tile,
        out_shape=jax.ShapeDtypeStruct(aq.shape, BF16),
        scratch_shapes=[pltpu.VMEM((2 * tq, LANES), BF16),
                        pltpu.VMEM((2 * tq, LANES), F32), pltpu.VMEM((2 * tq, LANES), F32),
                        pltpu.VMEM((2 * tq, LANES), F32),
                        pltpu.VMEM((2 * tq, tq), F32), pltpu.VMEM((2 * tq, tq), F32),
                        pltpu.VMEM((2 * tq, tq), BF16), pltpu.VMEM((2 * tq, tq), BF16),
                        pltpu.VMEM((2 * tq, LANES), F32), pltpu.VMEM((2 * tq, LANES), F32)],
        compiler_params=pltpu.CompilerParams(
            dimension_semantics=("arbitrary", "arbitrary", "arbitrary"),
            vmem_limit_bytes=VMEM_LIMIT),
        name="diff_attention",
    )(attn_lambda, subln_g, aq, ak, av, az)


def _ret_kernel(q_ref, k_ref, v_ref, z_ref, dm_ref, xi_ref, zeta_ref, dec_ref, o_ref, st_ref):
    @pl.when(pl.program_id(1) == 0)
    def _():
        st_ref[...] = jnp.zeros(st_ref.shape, F32)

    for hh in range(R_HEADS):
        sl = slice(hh * LANES, (hh + 1) * LANES)
        q = q_ref[:, sl]
        k = k_ref[:, sl]
        v = v_ref[:, sl]
        s = lax.dot_general(q, k, (((1,), (1,)), ((), ())), preferred_element_type=F32)
        inner = jnp.dot((s * dm_ref[hh]).astype(BF16), v, preferred_element_type=F32)
        st = st_ref[hh]
        cross = jnp.dot(q, st.astype(BF16), preferred_element_type=F32) * xi_ref[hh]
        o = _rms(inner + cross)
        o_ref[:, sl] = (o * z_ref[:, sl].astype(F32)).astype(BF16)
        kz = (k.astype(F32) * zeta_ref[hh]).astype(BF16)
        kv = lax.dot_general(kz, v, (((0,), (0,)), ((), ())), preferred_element_type=F32)
        st_ref[hh] = dec_ref[hh] * st + kv


def _retention(rq, rk, rv, rz, dmask, xi, zeta, dec):
    b, s, w = rq.shape
    c = RET_CHUNK
    tile = pl.BlockSpec((None, c, w), lambda bi, ti: (bi, ti, 0))
    const3 = lambda bi, ti: (0, 0, 0)
    return pl.pallas_call(
        _ret_kernel,
        grid=(b, s // c),
        in_specs=[tile, tile, tile, tile,
                  pl.BlockSpec(dmask.shape, const3), pl.BlockSpec(xi.shape, const3),
                  pl.BlockSpec(zeta.shape, const3), pl.BlockSpec(dec.shape, const3)],
        out_specs=tile,
        out_shape=jax.ShapeDtypeStruct(rq.shape, BF16),
        scratch_shapes=[pltpu.VMEM((R_HEADS, LANES, LANES), F32)],
        compiler_params=pltpu.CompilerParams(
            dimension_semantics=("arbitrary", "arbitrary"), vmem_limit_bytes=VMEM_LIMIT),
        name="retention",
    )(rq, rk, rv, rz, dmask, xi, zeta, dec)


def _merge_kernel(x_ref, a_ref, c_ref, r_ref, g_ref, wg_ref, wb_ref, wo_ref, fg_ref, o_ref,
                  *, final_norm):
    x = x_ref[...]
    d = x.shape[1]
    h = (_rms(x) * g_ref[...]).astype(BF16)
    merged = None
    for bi, br_ref in enumerate((a_ref, c_ref, r_ref)):
        t = jnp.dot(h, wg_ref[:, bi * d:(bi + 1) * d], preferred_element_type=F32)
        gate = 1.0 / (1.0 + jnp.exp(-t))
        contrib = gate * jnp.dot(br_ref[...], wb_ref[bi], preferred_element_type=F32)
        merged = contrib if merged is None else merged + contrib
    y = x + jnp.dot(merged.astype(BF16), wo_ref[...], preferred_element_type=F32)
    if final_norm:
        y = _rms(y) * fg_ref[...]
    o_ref[...] = y


def _merge(x2, a, c, r, norm_g, wg, wb, wo, final_g, final_norm):
    n, d = x2.shape
    tm = PROJ_ROWS
    row = lambda i: (i, 0)
    const = lambda i: (0, 0)
    br = pl.BlockSpec((tm, GROUP), row)
    return pl.pallas_call(
        functools.partial(_merge_kernel, final_norm=final_norm),
        grid=(n // tm,),
        in_specs=[pl.BlockSpec((tm, d), row), br, br, br,
                  pl.BlockSpec((1, d), const),
                  pl.BlockSpec(wg.shape, const),
                  pl.BlockSpec(wb.shape, lambda i: (0, 0, 0)),
                  pl.BlockSpec(wo.shape, const),
                  pl.BlockSpec((1, d), const)],
        out_specs=pl.BlockSpec((tm, d), row),
        out_shape=jax.ShapeDtypeStruct((n, d), F32),
        compiler_params=pltpu.CompilerParams(
            dimension_semantics=("arbitrary",), vmem_limit_bytes=VMEM_LIMIT),
        name="merge_outproj",
    )(x2, a, c, r, norm_g, wg, wb, wo, final_g)


def _attn_rope_table(seq):
    pos = jnp.arange(seq, dtype=F32)
    inv = ROPE_THETA ** (-jnp.arange(0, ROPE_DIM, 2, dtype=F32) / ROPE_DIM)
    ang = pos[:, None] * inv[None, :]
    cos, sin = jnp.cos(ang), jnp.sin(ang)
    half = ROPE_DIM // 2
    dd = jnp.arange(LANES) % A_QK_DIM
    j = dd % half
    c = jnp.where(dd[None, :] < ROPE_DIM, cos[:, j], 1.0)
    s_up = jnp.where(dd[None, :] < half, -sin[:, j], 0.0)
    s_dn = jnp.where((dd[None, :] >= half) & (dd[None, :] < ROPE_DIM), sin[:, j], 0.0)
    k_tab = jnp.concatenate([c, s_up, s_dn], axis=1)
    q_scale = (A_QK_DIM ** -0.5) * math.log2(math.e)
    return jnp.concatenate([k_tab * q_scale, k_tab], axis=1).astype(F32)


def _ret_rope_table(seq):
    pos = jnp.arange(seq, dtype=F32)
    inv = 1.0 / (RET_THETA ** jnp.linspace(0.0, 1.0, R_QK_DIM // 2, dtype=F32))
    ang = pos[:, None] * inv[None, :]
    cos, sin = jnp.cos(ang), jnp.sin(ang)
    half = R_QK_DIM // 2
    dd = jnp.arange(LANES) % R_QK_DIM
    j = dd % half
    c = cos[:, j]
    s_up = jnp.where(dd[None, :] < half, -sin[:, j], 0.0)
    s_dn = jnp.where(dd[None, :] >= half, sin[:, j], 0.0)
    scale = jnp.where(jnp.arange(LANES) < R_QK_DIM, 1.0, R_QK_DIM ** -0.5)[None, :]
    return jnp.concatenate([c * scale, s_up * scale, s_dn * scale], axis=1).astype(F32)


def _ret_decay_tables(chunk):
    log_g = jnp.log(1.0 - 2.0 ** (-5.0 - jnp.arange(R_HEADS, dtype=F32)))
    idx = jnp.arange(chunk, dtype=F32)
    diff = idx[:, None] - idx[None, :]
    dmask = jnp.where(diff >= 0,
                      jnp.exp(jnp.where(diff >= 0, diff, 0.0)[None] * log_g[:, None, None]),
                      0.0)
    ones = jnp.ones((1, 1, LANES), F32)
    xi = jnp.exp((idx + 1.0)[None, :] * log_g[:, None])[:, :, None] * ones
    zeta = jnp.exp((chunk - 1 - idx)[None, :] * log_g[:, None])[:, :, None] * ones
    dec = jnp.exp(chunk * log_g)[:, None, None] * ones
    return dmask.astype(F32), xi.astype(F32), zeta.astype(F32), dec.astype(F32)


def _split_w_in(w_in_l):
    d = w_in_l.shape[0]
    g = GROUP
    mix = w_in_l[:, :8 * g]
    rq = w_in_l[:, 8 * g:8 * g + g // 2].reshape(d, R_HEADS, R_QK_DIM)
    rk = w_in_l[:, 8 * g + g // 2:9 * g].reshape(d, R_HEADS, R_QK_DIM)
    rqk = jnp.concatenate([rq, rk], axis=2).reshape(d, g)
    rest = w_in_l[:, 9 * g:11 * g]
    w1 = jnp.concatenate([mix, rqk, rest], axis=1).astype(BF16)
    wg = w_in_l[:, 11 * g:].astype(BF16)
    return w1, wg


def kernel(x, norm_g, w_in, attn_lambda, attn_subln_g, conv_w, w_branch, w_out, final_norm_g):
    b, s, d = x.shape
    depth = w_in.shape[0]
    x2 = x.reshape(b * s, d)
    atab = _attn_rope_table(s)
    rtab = _ret_rope_table(s)
    dmask, xi, zeta, dec = _ret_decay_tables(RET_CHUNK)
    final_g = final_norm_g.reshape(1, d)
    for layer in range(depth):
        lam_init = 0.8 - 0.6 * math.exp(-0.3 * layer)
        w1, wg = _split_w_in(w_in[layer])
        g = norm_g[layer].reshape(1, d)
        aq, ak, av, az, c, rq, rk, rv, rz = _inproj(x2, g, w1, atab, rtab, conv_w[layer], s)
        sh = (b, s, GROUP)
        a = _attention(aq.reshape(sh), ak.reshape(sh), av.reshape(sh), az.reshape(sh),
                       attn_lambda[layer], attn_subln_g[layer].reshape(1, LANES), lam_init)
        r = _retention(rq.reshape(sh), rk.reshape(sh), rv.reshape(sh), rz.reshape(sh),
                       dmask, xi, zeta, dec)
        x2 = _merge(x2, a.reshape(b * s, GROUP), c, r.reshape(b * s, GROUP), g, wg,
                    w_branch[layer].astype(BF16), w_out[layer].astype(BF16), final_g,
                    layer == depth - 1)
    return x2.reshape(b, s, d)
```

```python
import functools
import math

import jax
import jax.numpy as jnp
from jax import lax
from jax.experimental import pallas as pl
from jax.experimental.pallas import tpu as pltpu

F32 = jnp.float32
BF16 = jnp.bfloat16

A_HEADS = 4
A_QK_DIM = 64
ROPE_THETA = 500000.0
ROPE_DIM = A_QK_DIM // 4
NEG_INF = -1e30
CONV_WIDTH = 3
R_HEADS = 4
R_QK_DIM = 64
RET_THETA = 10000.0
N_BRANCH = 3
EPS = 1e-6

LANES = 128
GROUP = 512
VMEM_LIMIT = 56 * 1024 * 1024

PROJ_ROWS = 512
ATT_TILE = 512
ROW_BLOCK = 64
RET_CHUNK = 512


def _silu(t):
    return t * (1.0 / (1.0 + jnp.exp(-t)))


def _rms(x):
    return x * lax.rsqrt(jnp.mean(x * x, axis=-1, keepdims=True) + EPS)


def _rope_block(blk, c, s_up, s_dn, shift):
    return (blk * c + pltpu.roll(blk, LANES - shift, 1) * s_up
            + pltpu.roll(blk, shift, 1) * s_dn)


def _inproj_kernel(x_ref, g_ref, w_ref, atab_ref, rtab_ref, cw_ref,
                   aq_ref, ak_ref, av_ref, az_ref, c_ref,
                   rq_ref, rk_ref, rv_ref, rz_ref, ubuf_ref, *, tiles_per_seq):
    i = pl.program_id(0)
    tm = x_ref.shape[0]
    h = (_rms(x_ref[...]) * g_ref[...]).astype(BF16)

    def proj(gi):
        return jnp.dot(h, w_ref[:, gi * GROUP:(gi + 1) * GROUP], preferred_element_type=F32)

    for gi, out_ref, t0 in ((0, aq_ref, 0), (1, ak_ref, 3 * LANES)):
        y = proj(gi)
        c = atab_ref[:, t0:t0 + LANES]
        s_up = atab_ref[:, t0 + LANES:t0 + 2 * LANES]
        s_dn = atab_ref[:, t0 + 2 * LANES:t0 + 3 * LANES]
        for hh in range(A_HEADS):
            sl = slice(hh * LANES, (hh + 1) * LANES)
            out_ref[:, sl] = _rope_block(y[:, sl], c, s_up, s_dn, ROPE_DIM // 2).astype(BF16)
    av_ref[...] = proj(2).astype(BF16)
    az_ref[...] = _silu(proj(3)).astype(BF16)

    u = proj(4) * proj(6)

    @pl.when(i % tiles_per_seq == 0)
    def _():
        ubuf_ref[0:8, :] = jnp.zeros((8, GROUP), F32)

    ubuf_ref[8:8 + tm, :] = u
    conv = (cw_ref[0:1, :] * ubuf_ref[6:6 + tm, :] + cw_ref[1:2, :] * ubuf_ref[7:7 + tm, :]
            + cw_ref[2:3, :] * u)
    ubuf_ref[0:8, :] = u[tm - 8:tm, :]
    c_ref[...] = (proj(5) * conv * _silu(proj(7))).astype(BF16)

    y = proj(8)
    c = rtab_ref[:, 0:LANES]
    s_up = rtab_ref[:, LANES:2 * LANES]
    s_dn = rtab_ref[:, 2 * LANES:3 * LANES]
    lane = lax.broadcasted_iota(jnp.int32, (tm, LANES), 1)
    low = lane < R_QK_DIM
    for hh in range(R_HEADS):
        sl = slice(hh * LANES, (hh + 1) * LANES)
        rot = _rope_block(y[:, sl], c, s_up, s_dn, R_QK_DIM // 2)
        rq_ref[:, sl] = jnp.where(low, rot, 0.0).astype(BF16)
        rk_ref[:, sl] = jnp.where(low, pltpu.roll(rot, R_QK_DIM, 1), 0.0).astype(BF16)
    rv_ref[...] = proj(9).astype(BF16)
    rz_ref[...] = _silu(proj(10)).astype(BF16)


def _inproj(x2, norm_g, w1, atab, rtab, conv_w, seq):
    n, d = x2.shape
    tm = PROJ_ROWS
    tiles_per_seq = seq // tm
    row = lambda i: (i, 0)
    pos = lambda i: (i % tiles_per_seq, 0)
    const = lambda i: (0, 0)
    out_sds = jax.ShapeDtypeStruct((n, GROUP), BF16)
    out_spec = pl.BlockSpec((tm, GROUP), row)
    return pl.pallas_call(
        functools.partial(_inproj_kernel, tiles_per_seq=tiles_per_seq),
        grid=(n // tm,),
        in_specs=[
            pl.BlockSpec((tm, d), row),
            pl.BlockSpec((1, d), const),
            pl.BlockSpec(w1.shape, const),
            pl.BlockSpec((tm, atab.shape[1]), pos),
            pl.BlockSpec((tm, rtab.shape[1]), pos),
            pl.BlockSpec(conv_w.shape, const),
        ],
        out_specs=[out_spec] * 9,
        out_shape=[out_sds] * 9,
        scratch_shapes=[pltpu.VMEM((tm + 8, GROUP), F32)],
        compiler_params=pltpu.CompilerParams(
            dimension_semantics=("arbitrary",), vmem_limit_bytes=VMEM_LIMIT),
        name="inproj",
    )(x2, norm_g, w1, atab, rtab, conv_w)


def _attn_kernel(lam_ref, g_ref, q_ref, k_ref, v_ref, z_ref, o_ref,
                 qs_ref, m_ref, l_ref, acc_ref, *, lam_init):
    i = pl.program_id(2)
    tq = q_ref.shape[0]
    nchunk = tq // LANES
    q = q_ref[...]
    lane = lax.broadcasted_iota(jnp.int32, q.shape, 1)
    zero = jnp.zeros_like(q)
    qs_ref[0:tq, :] = jnp.where(lane < A_QK_DIM, q, zero)
    qs_ref[tq:2 * tq, :] = jnp.where(lane >= A_QK_DIM, q, zero)
    m_ref[...] = jnp.full(m_ref.shape, NEG_INF, F32)
    l_ref[...] = jnp.zeros(l_ref.shape, F32)
    acc_ref[...] = jnp.zeros(acc_ref.shape, F32)

    def step(j, masked):
        rows = pl.ds(pl.multiple_of(j * tq, tq), tq)
        s = lax.dot_general(qs_ref[...], k_ref[rows, :], (((1,), (1,)), ((), ())),
                            preferred_element_type=F32)
        if masked:
            r = lax.broadcasted_iota(jnp.int32, s.shape, 0)
            cidx = lax.broadcasted_iota(jnp.int32, s.shape, 1)
            s = jnp.where(cidx <= jnp.where(r >= tq, r - tq, r), s, NEG_INF)
        m_prev = m_ref[...]
        m_new = jnp.maximum(m_prev, jnp.max(s, axis=1, keepdims=True))
        alpha = jnp.exp2(m_prev - m_new)
        m_ref[...] = m_new
        psum = None
        pcs = []
        for c in range(nchunk):
            pc = jnp.exp2(s[:, c * LANES:(c + 1) * LANES] - m_new)
            psum = pc if psum is None else psum + pc
            pcs.append(pc.astype(BF16))
        l_ref[...] = alpha * l_ref[...] + psum
        acc_ref[...] = alpha * acc_ref[...] + jnp.dot(
            jnp.concatenate(pcs, axis=1), v_ref[rows, :], preferred_element_type=F32)

    def pair(kk, carry):
        step(2 * kk, False)
        step(2 * kk + 1, False)
        return carry

    lax.fori_loop(0, i // 2, pair, 0)

    @pl.when(i % 2 == 1)
    def _():
        step(i - 1, False)

    step(i, True)

    o = acc_ref[...] / jnp.sum(l_ref[...], axis=1, keepdims=True)
    al = lam_ref[...]
    lam = (jnp.exp(jnp.sum(al[0:1] * al[1:2], axis=1, keepdims=True))
           - jnp.exp(jnp.sum(al[2:3] * al[3:4], axis=1, keepdims=True)) + lam_init)
    d = o[:tq] - lam * o[tq:]
    d = _rms(d) * g_ref[...] * (1.0 - lam_init)
    o_ref[...] = (d * z_ref[...].astype(F32)).astype(BF16)


def _attention(aq, ak, av, az, attn_lambda, subln_g, lam_init):
    b, s, _ = aq.shape
    tq = ATT_TILE
    tile = pl.BlockSpec((None, tq, LANES), lambda bi, hi, qi: (bi, qi, hi))
    full = pl.BlockSpec((None, s, LANES), lambda bi, hi, qi: (bi, 0, hi))
    const = lambda bi, hi, qi: (0, 0)
    return pl.pallas_call(
        functools.partial(_attn_kernel, lam_init=lam_init),
        grid=(b, A_HEADS, s // tq),
        in_specs=[
            pl.BlockSpec(attn_lambda.shape, const),
            pl.BlockSpec(subln_g.shape, const),
            tile, full, full, tile,
        ],
        out_specs=tile,
        out_shape=jax.ShapeDtypeStruct(aq.shape, BF16),
        scratch_shapes=[pltpu.VMEM((2 * tq, LANES), BF16),
                        pltpu.VMEM((2 * tq, LANES), F32), pltpu.VMEM((2 * tq, LANES), F32),
                        pltpu.VMEM((2 * tq, LANES), F32)],
        compiler_params=pltpu.CompilerParams(
            dimension_semantics=("arbitrary", "arbitrary", "arbitrary"),
            vmem_limit_bytes=VMEM_LIMIT),
        name="diff_attention",
    )(attn_lambda, subln_g, aq, ak, av, az)


def _ret_kernel(q_ref, k_ref, v_ref, z_ref, dm_ref, xi_ref, zeta_ref, dec_ref, o_ref, st_ref):
    @pl.when(pl.program_id(1) == 0)
    def _():
        st_ref[...] = jnp.zeros(st_ref.shape, F32)

    for hh in range(R_HEADS):
        sl = slice(hh * LANES, (hh + 1) * LANES)
        q = q_ref[:, sl]
        k = k_ref[:, sl]
        v = v_ref[:, sl]
        s = lax.dot_general(q, k, (((1,), (1,)), ((), ())), preferred_element_type=F32)
        inner = jnp.dot((s * dm_ref[hh]).astype(BF16), v, preferred_element_type=F32)
        st = st_ref[hh]
        cross = jnp.dot(q, st.astype(BF16), preferred_element_type=F32) * xi_ref[hh]
        o = _rms(inner + cross)
        o_ref[:, sl] = (o * z_ref[:, sl].astype(F32)).astype(BF16)
        kz = (k.astype(F32) * zeta_ref[hh]).astype(BF16)
        kv = lax.dot_general(kz, v, (((0,), (0,)), ((), ())), preferred_element_type=F32)
        st_ref[hh] = dec_ref[hh] * st + kv


def _retention(rq, rk, rv, rz, dmask, xi, zeta, dec):
    b, s, w = rq.shape
    c = RET_CHUNK
    tile = pl.BlockSpec((None, c, w), lambda bi, ti: (bi, ti, 0))
    const3 = lambda bi, ti: (0, 0, 0)
    return pl.pallas_call(
        _ret_kernel,
        grid=(b, s // c),
        in_specs=[tile, tile, tile, tile,
                  pl.BlockSpec(dmask.shape, const3), pl.BlockSpec(xi.shape, const3),
                  pl.BlockSpec(zeta.shape, const3), pl.BlockSpec(dec.shape, const3)],
        out_specs=tile,
        out_shape=jax.ShapeDtypeStruct(rq.shape, BF16),
        scratch_shapes=[pltpu.VMEM((R_HEADS, LANES, LANES), F32)],
        compiler_params=pltpu.CompilerParams(
            dimension_semantics=("arbitrary", "arbitrary"), vmem_limit_bytes=VMEM_LIMIT),
        name="retention",
    )(rq, rk, rv, rz, dmask, xi, zeta, dec)


def _merge_kernel(x_ref, a_ref, c_ref, r_ref, g_ref, wg_ref, wb_ref, wo_ref, fg_ref, o_ref,
                  *, final_norm):
    x = x_ref[...]
    d = x.shape[1]
    h = (_rms(x) * g_ref[...]).astype(BF16)
    merged = None
    for bi, br_ref in enumerate((a_ref, c_ref, r_ref)):
        t = jnp.dot(h, wg_ref[:, bi * d:(bi + 1) * d], preferred_element_type=F32)
        gate = 1.0 / (1.0 + jnp.exp(-t))
        contrib = gate * jnp.dot(br_ref[...], wb_ref[bi], preferred_element_type=F32)
        merged = contrib if merged is None else merged + contrib
    y = x + jnp.dot(merged.astype(BF16), wo_ref[...], preferred_element_type=F32)
    if final_norm:
        y = _rms(y) * fg_ref[...]
    o_ref[...] = y


def _merge(x2, a, c, r, norm_g, wg, wb, wo, final_g, final_norm):
    n, d = x2.shape
    tm = PROJ_ROWS
    row = lambda i: (i, 0)
    const = lambda i: (0, 0)
    br = pl.BlockSpec((tm, GROUP), row)
    return pl.pallas_call(
        functools.partial(_merge_kernel, final_norm=final_norm),
        grid=(n // tm,),
        in_specs=[pl.BlockSpec((tm, d), row), br, br, br,
                  pl.BlockSpec((1, d), const),
                  pl.BlockSpec(wg.shape, const),
                  pl.BlockSpec(wb.shape, lambda i: (0, 0, 0)),
                  pl.BlockSpec(wo.shape, const),
                  pl.BlockSpec((1, d), const)],
        out_specs=pl.BlockSpec((tm, d), row),
        out_shape=jax.ShapeDtypeStruct((n, d), F32),
        compiler_params=pltpu.CompilerParams(
            dimension_semantics=("arbitrary",), vmem_limit_bytes=VMEM_LIMIT),
        name="merge_outproj",
    )(x2, a, c, r, norm_g, wg, wb, wo, final_g)


def _attn_rope_table(seq):
    pos = jnp.arange(seq, dtype=F32)
    inv = ROPE_THETA ** (-jnp.arange(0, ROPE_DIM, 2, dtype=F32) / ROPE_DIM)
    ang = pos[:, None] * inv[None, :]
    cos, sin = jnp.cos(ang), jnp.sin(ang)
    half = ROPE_DIM // 2
    dd = jnp.arange(LANES) % A_QK_DIM
    j = dd % half
    c = jnp.where(dd[None, :] < ROPE_DIM, cos[:, j], 1.0)
    s_up = jnp.where(dd[None, :] < half, -sin[:, j], 0.0)
    s_dn = jnp.where((dd[None, :] >= half) & (dd[None, :] < ROPE_DIM), sin[:, j], 0.0)
    k_tab = jnp.concatenate([c, s_up, s_dn], axis=1)
    q_scale = (A_QK_DIM ** -0.5) * math.log2(math.e)
    return jnp.concatenate([k_tab * q_scale, k_tab], axis=1).astype(F32)


def _ret_rope_table(seq):
    pos = jnp.arange(seq, dtype=F32)
    inv = 1.0 / (RET_THETA ** jnp.linspace(0.0, 1.0, R_QK_DIM // 2, dtype=F32))
    ang = pos[:, None] * inv[None, :]
    cos, sin = jnp.cos(ang), jnp.sin(ang)
    half = R_QK_DIM // 2
    dd = jnp.arange(LANES) % R_QK_DIM
    j = dd % half
    c = cos[:, j]
    s_up = jnp.where(dd[None, :] < half, -sin[:, j], 0.0)
    s_dn = jnp.where(dd[None, :] >= half, sin[:, j], 0.0)
    scale = jnp.where(jnp.arange(LANES) < R_QK_DIM, 1.0, R_QK_DIM ** -0.5)[None, :]
    return jnp.concatenate([c * scale, s_up * scale, s_dn * scale], axis=1).astype(F32)


def _ret_decay_tables(chunk):
    log_g = jnp.log(1.0 - 2.0 ** (-5.0 - jnp.arange(R_HEADS, dtype=F32)))
    idx = jnp.arange(chunk, dtype=F32)
    diff = idx[:, None] - idx[None, :]
    dmask = jnp.where(diff >= 0,
                      jnp.exp(jnp.where(diff >= 0, diff, 0.0)[None] * log_g[:, None, None]),
                      0.0)
    ones = jnp.ones((1, 1, LANES), F32)
    xi = jnp.exp((idx + 1.0)[None, :] * log_g[:, None])[:, :, None] * ones
    zeta = jnp.exp((chunk - 1 - idx)[None, :] * log_g[:, None])[:, :, None] * ones
    dec = jnp.exp(chunk * log_g)[:, None, None] * ones
    return dmask.astype(F32), xi.astype(F32), zeta.astype(F32), dec.astype(F32)


def _split_w_in(w_in_l):
    d = w_in_l.shape[0]
    g = GROUP
    mix = w_in_l[:, :8 * g]
    rq = w_in_l[:, 8 * g:8 * g + g // 2].reshape(d, R_HEADS, R_QK_DIM)
    rk = w_in_l[:, 8 * g + g // 2:9 * g].reshape(d, R_HEADS, R_QK_DIM)
    rqk = jnp.concatenate([rq, rk], axis=2).reshape(d, g)
    rest = w_in_l[:, 9 * g:11 * g]
    w1 = jnp.concatenate([mix, rqk, rest], axis=1).astype(BF16)
    wg = w_in_l[:, 11 * g:].astype(BF16)
    return w1, wg


def kernel(x, norm_g, w_in, attn_lambda, attn_subln_g, conv_w, w_branch, w_out, final_norm_g):
    b, s, d = x.shape
    depth = w_in.shape[0]
    x2 = x.reshape(b * s, d)
    atab = _attn_rope_table(s)
    rtab = _ret_rope_table(s)
    dmask, xi, zeta, dec = _ret_decay_tables(RET_CHUNK)
    final_g = final_norm_g.reshape(1, d)
    for layer in range(depth):
        lam_init = 0.8 - 0.6 * math.exp(-0.3 * layer)
        w1, wg = _split_w_in(w_in[layer])
        g = norm_g[layer].reshape(1, d)
        aq, ak, av, az, c, rq, rk, rv, rz = _inproj(x2, g, w1, atab, rtab, conv_w[layer], s)
        sh = (b, s, GROUP)
        a = _attention(aq.reshape(sh), ak.reshape(sh), av.reshape(sh), az.reshape(sh),
                       attn_lambda[layer], attn_subln_g[layer].reshape(1, LANES), lam_init)
        r = _retention(rq.reshape(sh), rk.reshape(sh), rv.reshape(sh), rz.reshape(sh),
                       dmask, xi, zeta, dec)
        x2 = _merge(x2, a.reshape(b * s, GROUP), c, r.reshape(b * s, GROUP), g, wg,
                    w_branch[layer].astype(BF16), w_out[layer].astype(BF16), final_g,
                    layer == depth - 1)
    return x2.reshape(b, s, d)
```

```python
import functools
import math

import jax
import jax.numpy as jnp
from jax import lax
from jax.experimental import pallas as pl
from jax.experimental.pallas import tpu as pltpu

F32 = jnp.float32
BF16 = jnp.bfloat16

A_HEADS = 4
A_QK_DIM = 64
ROPE_THETA = 500000.0
ROPE_DIM = A_QK_DIM // 4
NEG_INF = -1e30
CONV_WIDTH = 3
R_HEADS = 4
R_QK_DIM = 64
RET_THETA = 10000.0
N_BRANCH = 3
EPS = 1e-6

LANES = 128
GROUP = 512
VMEM_LIMIT = 56 * 1024 * 1024

PROJ_ROWS = 512
ATT_TILE = 512
RET_CHUNK = 512


def _silu(t):
    return t * (1.0 / (1.0 + jnp.exp(-t)))


def _rms(x):
    return x * lax.rsqrt(jnp.mean(x * x, axis=-1, keepdims=True) + EPS)


def _rope_block(blk, c, s_up, s_dn, shift):
    return (blk * c + pltpu.roll(blk, LANES - shift, 1) * s_up
            + pltpu.roll(blk, shift, 1) * s_dn)


def _inproj_kernel(x_ref, g_ref, w_ref, atab_ref, rtab_ref, cw_ref,
                   aq_ref, ak_ref, av_ref, az_ref, c_ref,
                   rq_ref, rk_ref, rv_ref, rz_ref, ubuf_ref, *, tiles_per_seq):
    i = pl.program_id(0)
    tm = x_ref.shape[0]
    h = (_rms(x_ref[...]) * g_ref[...]).astype(BF16)

    def proj(gi):
        return jnp.dot(h, w_ref[:, gi * GROUP:(gi + 1) * GROUP], preferred_element_type=F32)

    for gi, out_ref, t0 in ((0, aq_ref, 0), (1, ak_ref, 3 * LANES)):
        y = proj(gi)
        c = atab_ref[:, t0:t0 + LANES]
        s_up = atab_ref[:, t0 + LANES:t0 + 2 * LANES]
        s_dn = atab_ref[:, t0 + 2 * LANES:t0 + 3 * LANES]
        for hh in range(A_HEADS):
            sl = slice(hh * LANES, (hh + 1) * LANES)
            out_ref[:, sl] = _rope_block(y[:, sl], c, s_up, s_dn, ROPE_DIM // 2).astype(BF16)
    av_ref[...] = proj(2).astype(BF16)
    az_ref[...] = _silu(proj(3)).astype(BF16)

    u = proj(4) * proj(6)

    @pl.when(i % tiles_per_seq == 0)
    def _():
        ubuf_ref[0:8, :] = jnp.zeros((8, GROUP), F32)

    ubuf_ref[8:8 + tm, :] = u
    conv = (cw_ref[0:1, :] * ubuf_ref[6:6 + tm, :] + cw_ref[1:2, :] * ubuf_ref[7:7 + tm, :]
            + cw_ref[2:3, :] * u)
    ubuf_ref[0:8, :] = u[tm - 8:tm, :]
    c_ref[...] = (proj(5) * conv * _silu(proj(7))).astype(BF16)

    y = proj(8)
    c = rtab_ref[:, 0:LANES]
    s_up = rtab_ref[:, LANES:2 * LANES]
    s_dn = rtab_ref[:, 2 * LANES:3 * LANES]
    lane = lax.broadcasted_iota(jnp.int32, (tm, LANES), 1)
    low = lane < R_QK_DIM
    for hh in range(R_HEADS):
        sl = slice(hh * LANES, (hh + 1) * LANES)
        rot = _rope_block(y[:, sl], c, s_up, s_dn, R_QK_DIM // 2)
        rq_ref[:, sl] = jnp.where(low, rot, 0.0).astype(BF16)
        rk_ref[:, sl] = jnp.where(low, pltpu.roll(rot, R_QK_DIM, 1), 0.0).astype(BF16)
    rv_ref[...] = proj(9).astype(BF16)
    rz_ref[...] = _silu(proj(10)).astype(BF16)


def _inproj(x2, norm_g, w1, atab, rtab, conv_w, seq):
    n, d = x2.shape
    tm = PROJ_ROWS
    tiles_per_seq = seq // tm
    row = lambda i: (i, 0)
    pos = lambda i: (i % tiles_per_seq, 0)
    const = lambda i: (0, 0)
    out_sds = jax.ShapeDtypeStruct((n, GROUP), BF16)
    out_spec = pl.BlockSpec((tm, GROUP), row)
    return pl.pallas_call(
        functools.partial(_inproj_kernel, tiles_per_seq=tiles_per_seq),
        grid=(n // tm,),
        in_specs=[
            pl.BlockSpec((tm, d), row),
            pl.BlockSpec((1, d), const),
            pl.BlockSpec(w1.shape, const),
            pl.BlockSpec((tm, atab.shape[1]), pos),
            pl.BlockSpec((tm, rtab.shape[1]), pos),
            pl.BlockSpec(conv_w.shape, const),
        ],
        out_specs=[out_spec] * 9,
        out_shape=[out_sds] * 9,
        scratch_shapes=[pltpu.VMEM((tm + 8, GROUP), F32)],
        compiler_params=pltpu.CompilerParams(
            dimension_semantics=("arbitrary",), vmem_limit_bytes=VMEM_LIMIT),
        name="inproj",
    )(x2, norm_g, w1, atab, rtab, conv_w)


def _attn_kernel(lam_ref, g_ref, q_ref, k_ref, v_ref, z_ref, o_ref,
                 qs_ref, m_ref, l_ref, acc_ref, *, lam_init):
    i = pl.program_id(2)
    tq = q_ref.shape[0]
    q = q_ref[...]
    lane = lax.broadcasted_iota(jnp.int32, q.shape, 1)
    zero = jnp.zeros_like(q)
    qs_ref[0:tq, :] = jnp.where(lane < A_QK_DIM, q, zero)
    qs_ref[tq:2 * tq, :] = jnp.where(lane >= A_QK_DIM, q, zero)
    m_ref[...] = jnp.full(m_ref.shape, NEG_INF, F32)
    l_ref[...] = jnp.zeros(l_ref.shape, F32)
    acc_ref[...] = jnp.zeros(acc_ref.shape, F32)

    def step(j, width, masked=False):
        rows = pl.ds(pl.multiple_of(j * tq, tq), width)
        s = lax.dot_general(qs_ref[...], k_ref[rows, :], (((1,), (1,)), ((), ())),
                            preferred_element_type=F32)
        if masked:
            r = lax.broadcasted_iota(jnp.int32, s.shape, 0)
            cidx = lax.broadcasted_iota(jnp.int32, s.shape, 1)
            s = jnp.where(cidx <= jnp.where(r >= tq, r - tq, r), s, NEG_INF)
        m_prev = m_ref[...]
        m_new = jnp.maximum(m_prev, jnp.max(s, axis=1, keepdims=True))
        alpha = jnp.exp2(m_prev - m_new)
        m_ref[...] = m_new
        psum = None
        pcs = []
        for c in range(width // LANES):
            pc = jnp.exp2(s[:, c * LANES:(c + 1) * LANES] - m_new)
            psum = pc if psum is None else psum + pc
            pcs.append(pc.astype(BF16))
        l_ref[...] = alpha * l_ref[...] + psum
        acc_ref[...] = alpha * acc_ref[...] + jnp.dot(
            jnp.concatenate(pcs, axis=1), v_ref[rows, :], preferred_element_type=F32)

    def wide(kk, carry):
        step(2 * kk, 2 * tq)
        return carry

    lax.fori_loop(0, i // 2, wide, 0)

    @pl.when(i % 2 == 1)
    def _():
        step(i - 1, tq)

    step(i, tq, masked=True)

    o = acc_ref[...] / jnp.sum(l_ref[...], axis=1, keepdims=True)
    al = lam_ref[...]
    lam = (jnp.exp(jnp.sum(al[0:1] * al[1:2], axis=1, keepdims=True))
           - jnp.exp(jnp.sum(al[2:3] * al[3:4], axis=1, keepdims=True)) + lam_init)
    d = o[:tq] - lam * o[tq:]
    d = _rms(d) * g_ref[...] * (1.0 - lam_init)
    o_ref[...] = (d * z_ref[...].astype(F32)).astype(BF16)


def _attention(aq, ak, av, az, attn_lambda, subln_g, lam_init):
    b, s, _ = aq.shape
    tq = ATT_TILE
    tile = pl.BlockSpec((None, tq, LANES), lambda bi, hi, qi: (bi, qi, hi))
    full = pl.BlockSpec((None, s, LANES), lambda bi, hi, qi: (bi, 0, hi))
    const = lambda bi, hi, qi: (0, 0)
    return pl.pallas_call(
        functools.partial(_attn_kernel, lam_init=lam_init),
        grid=(b, A_HEADS, s // tq),
        in_specs=[
            pl.BlockSpec(attn_lambda.shape, const),
            pl.BlockSpec(subln_g.shape, const),
            tile, full, full, tile,
        ],
        out_specs=tile,
        out_shape=jax.ShapeDtypeStruct(aq.shape, BF16),
        scratch_shapes=[pltpu.VMEM((2 * tq, LANES), BF16),
                        pltpu.VMEM((2 * tq, LANES), F32), pltpu.VMEM((2 * tq, LANES), F32),
                        pltpu.VMEM((2 * tq, LANES), F32)],
        compiler_params=pltpu.CompilerParams(
            dimension_semantics=("arbitrary", "arbitrary", "arbitrary"),
            vmem_limit_bytes=VMEM_LIMIT),
        name="diff_attention",
    )(attn_lambda, subln_g, aq, ak, av, az)


def _ret_kernel(q_ref, k_ref, v_ref, z_ref, dm_ref, xi_ref, zeta_ref, dec_ref, o_ref, st_ref):
    @pl.when(pl.program_id(1) == 0)
    def _():
        st_ref[...] = jnp.zeros(st_ref.shape, F32)

    for hh in range(R_HEADS):
        sl = slice(hh * LANES, (hh + 1) * LANES)
        q = q_ref[:, sl]
        k = k_ref[:, sl]
        v = v_ref[:, sl]
        s = lax.dot_general(q, k, (((1,), (1,)), ((), ())), preferred_element_type=F32)
        inner = jnp.dot((s * dm_ref[hh]).astype(BF16), v, preferred_element_type=F32)
        st = st_ref[hh]
        cross = jnp.dot(q, st.astype(BF16), preferred_element_type=F32) * xi_ref[hh]
        o = _rms(inner + cross)
        o_ref[:, sl] = (o * z_ref[:, sl].astype(F32)).astype(BF16)
        kz = (k.astype(F32) * zeta_ref[hh]).astype(BF16)
        kv = lax.dot_general(kz, v, (((0,), (0,)), ((), ())), preferred_element_type=F32)
        st_ref[hh] = dec_ref[hh] * st + kv


def _retention(rq, rk, rv, rz, dmask, xi, zeta, dec):
    b, s, w = rq.shape
    c = RET_CHUNK
    tile = pl.BlockSpec((None, c, w), lambda bi, ti: (bi, ti, 0))
    const3 = lambda bi, ti: (0, 0, 0)
    return pl.pallas_call(
        _ret_kernel,
        grid=(b, s // c),
        in_specs=[tile, tile, tile, tile,
                  pl.BlockSpec(dmask.shape, const3), pl.BlockSpec(xi.shape, const3),
                  pl.BlockSpec(zeta.shape, const3), pl.BlockSpec(dec.shape, const3)],
        out_specs=tile,
        out_shape=jax.ShapeDtypeStruct(rq.shape, BF16),
        scratch_shapes=[pltpu.VMEM((R_HEADS, LANES, LANES), F32)],
        compiler_params=pltpu.CompilerParams(
            dimension_semantics=("arbitrary", "arbitrary"), vmem_limit_bytes=VMEM_LIMIT),
        name="retention",
    )(rq, rk, rv, rz, dmask, xi, zeta, dec)


def _merge_kernel(x_ref, a_ref, c_ref, r_ref, g_ref, wg_ref, wb_ref, wo_ref, fg_ref, o_ref,
                  *, final_norm):
    x = x_ref[...]
    d = x.shape[1]
    h = (_rms(x) * g_ref[...]).astype(BF16)
    merged = None
    for bi, br_ref in enumerate((a_ref, c_ref, r_ref)):
        t = jnp.dot(h, wg_ref[:, bi * d:(bi + 1) * d], preferred_element_type=F32)
        gate = 1.0 / (1.0 + jnp.exp(-t))
        contrib = gate * jnp.dot(br_ref[...], wb_ref[bi], preferred_element_type=F32)
        merged = contrib if merged is None else merged + contrib
    y = x + jnp.dot(merged.astype(BF16), wo_ref[...], preferred_element_type=F32)
    if final_norm:
        y = _rms(y) * fg_ref[...]
    o_ref[...] = y


def _merge(x2, a, c, r, norm_g, wg, wb, wo, final_g, final_norm):
    n, d = x2.shape
    tm = PROJ_ROWS
    row = lambda i: (i, 0)
    const = lambda i: (0, 0)
    br = pl.BlockSpec((tm, GROUP), row)
    return pl.pallas_call(
        functools.partial(_merge_kernel, final_norm=final_norm),
        grid=(n // tm,),
        in_specs=[pl.BlockSpec((tm, d), row), br, br, br,
                  pl.BlockSpec((1, d), const),
                  pl.BlockSpec(wg.shape, const),
                  pl.BlockSpec(wb.shape, lambda i: (0, 0, 0)),
                  pl.BlockSpec(wo.shape, const),
                  pl.BlockSpec((1, d), const)],
        out_specs=pl.BlockSpec((tm, d), row),
        out_shape=jax.ShapeDtypeStruct((n, d), F32),
        compiler_params=pltpu.CompilerParams(
            dimension_semantics=("arbitrary",), vmem_limit_bytes=VMEM_LIMIT),
        name="merge_outproj",
    )(x2, a, c, r, norm_g, wg, wb, wo, final_g)


def _attn_rope_table(seq):
    pos = jnp.arange(seq, dtype=F32)
    inv = ROPE_THETA ** (-jnp.arange(0, ROPE_DIM, 2, dtype=F32) / ROPE_DIM)
    ang = pos[:, None] * inv[None, :]
    cos, sin = jnp.cos(ang), jnp.sin(ang)
    half = ROPE_DIM // 2
    dd = jnp.arange(LANES) % A_QK_DIM
    j = dd % half
    c = jnp.where(dd[None, :] < ROPE_DIM, cos[:, j], 1.0)
    s_up = jnp.where(dd[None, :] < half, -sin[:, j], 0.0)
    s_dn = jnp.where((dd[None, :] >= half) & (dd[None, :] < ROPE_DIM), sin[:, j], 0.0)
    k_tab = jnp.concatenate([c, s_up, s_dn], axis=1)
    q_scale = (A_QK_DIM ** -0.5) * math.log2(math.e)
    return jnp.concatenate([k_tab * q_scale, k_tab], axis=1).astype(F32)


def _ret_rope_table(seq):
    pos = jnp.arange(seq, dtype=F32)
    inv = 1.0 / (RET_THETA ** jnp.linspace(0.0, 1.0, R_QK_DIM // 2, dtype=F32))
    ang = pos[:, None] * inv[None, :]
    cos, sin = jnp.cos(ang), jnp.sin(ang)
    half = R_QK_DIM // 2
    dd = jnp.arange(LANES) % R_QK_DIM
    j = dd % half
    c = cos[:, j]
    s_up = jnp.where(dd[None, :] < half, -sin[:, j], 0.0)
    s_dn = jnp.where(dd[None, :] >= half, sin[:, j], 0.0)
    scale = jnp.where(jnp.arange(LANES) < R_QK_DIM, 1.0, R_QK_DIM ** -0.5)[None, :]
    return jnp.concatenate([c * scale, s_up * scale, s_dn * scale], axis=1).astype(F32)


def _ret_decay_tables(chunk):
    log_g = jnp.log(1.0 - 2.0 ** (-5.0 - jnp.arange(R_HEADS, dtype=F32)))
    idx = jnp.arange(chunk, dtype=F32)
    diff = idx[:, None] - idx[None, :]
    dmask = jnp.where(diff >= 0,
                      jnp.exp(jnp.where(diff >= 0, diff, 0.0)[None] * log_g[:, None, None]),
                      0.0)
    ones = jnp.ones((1, 1, LANES), F32)
    xi = jnp.exp((idx + 1.0)[None, :] * log_g[:, None])[:, :, None] * ones
    zeta = jnp.exp((chunk - 1 - idx)[None, :] * log_g[:, None])[:, :, None] * ones
    dec = jnp.exp(chunk * log_g)[:, None, None] * ones
    return dmask.astype(F32), xi.astype(F32), zeta.astype(F32), dec.astype(F32)


def _split_w_in(w_in_l):
    d = w_in_l.shape[0]
    g = GROUP
    mix = w_in_l[:, :8 * g]
    rq = w_in_l[:, 8 * g:8 * g + g // 2].reshape(d, R_HEADS, R_QK_DIM)
    rk = w_in_l[:, 8 * g + g // 2:9 * g].reshape(d, R_HEADS, R_QK_DIM)
    rqk = jnp.concatenate([rq, rk], axis=2).reshape(d, g)
    rest = w_in_l[:, 9 * g:11 * g]
    w1 = jnp.concatenate([mix, rqk, rest], axis=1).astype(BF16)
    wg = w_in_l[:, 11 * g:].astype(BF16)
    return w1, wg


def kernel(x, norm_g, w_in, attn_lambda, attn_subln_g, conv_w, w_branch, w_out, final_norm_g):
    b, s, d = x.shape
    depth = w_in.shape[0]
    x2 = x.reshape(b * s, d)
    atab = _attn_rope_table(s)
    rtab = _ret_rope_table(s)
    dmask, xi, zeta, dec = _ret_decay_tables(RET_CHUNK)
    final_g = final_norm_g.reshape(1, d)
    for layer in range(depth):
        lam_init = 0.8 - 0.6 * math.exp(-0.3 * layer)
        w1, wg = _split_w_in(w_in[layer])
        g = norm_g[layer].reshape(1, d)
        aq, ak, av, az, c, rq, rk, rv, rz = _inproj(x2, g, w1, atab, rtab, conv_w[layer], s)
        sh = (b, s, GROUP)
        a = _attention(aq.reshape(sh), ak.reshape(sh), av.reshape(sh), az.reshape(sh),
                       attn_lambda[layer], attn_subln_g[layer].reshape(1, LANES), lam_init)
        r = _retention(rq.reshape(sh), rk.reshape(sh), rv.reshape(sh), rz.reshape(sh),
                       dmask, xi, zeta, dec)
        x2 = _merge(x2, a.reshape(b * s, GROUP), c, r.reshape(b * s, GROUP), g, wg,
                    w_branch[layer].astype(BF16), w_out[layer].astype(BF16), final_g,
                    layer == depth - 1)
    return x2.reshape(b, s, d)
```

```python
import functools
import math

import jax
import jax.numpy as jnp
from jax import lax
from jax.experimental import pallas as pl
from jax.experimental.pallas import tpu as pltpu

F32 = jnp.float32
BF16 = jnp.bfloat16

A_HEADS = 4
A_QK_DIM = 64
ROPE_THETA = 500000.0
ROPE_DIM = A_QK_DIM // 4
NEG_INF = -1e30
CONV_WIDTH = 3
R_HEADS = 4
R_QK_DIM = 64
RET_THETA = 10000.0
N_BRANCH = 3
EPS = 1e-6

LANES = 128
GROUP = 512
VMEM_LIMIT = 56 * 1024 * 1024

PROJ_ROWS = 512
ATT_TILE = 512
RET_CHUNK = 512


def _silu(t):
    return t * (1.0 / (1.0 + jnp.exp(-t)))


def _rms(x):
    return x * lax.rsqrt(jnp.mean(x * x, axis=-1, keepdims=True) + EPS)


def _rope_block(blk, c, s_up, s_dn, shift):
    return (blk * c + pltpu.roll(blk, LANES - shift, 1) * s_up
            + pltpu.roll(blk, shift, 1) * s_dn)


def _inproj_kernel(x_ref, g_ref, w_ref, atab_ref, rtab_ref, cw_ref,
                   aq_ref, ak_ref, av_ref, az_ref, c_ref,
                   rq_ref, rk_ref, rv_ref, rz_ref, ubuf_ref, *, tiles_per_seq):
    i = pl.program_id(0)
    tm = x_ref.shape[0]
    h = (_rms(x_ref[...]) * g_ref[...]).astype(BF16)

    def proj(gi):
        return jnp.dot(h, w_ref[:, gi * GROUP:(gi + 1) * GROUP], preferred_element_type=F32)

    for gi, out_ref, t0 in ((0, aq_ref, 0), (1, ak_ref, 3 * LANES)):
        y = proj(gi)
        c = atab_ref[:, t0:t0 + LANES]
        s_up = atab_ref[:, t0 + LANES:t0 + 2 * LANES]
        s_dn = atab_ref[:, t0 + 2 * LANES:t0 + 3 * LANES]
        for hh in range(A_HEADS):
            sl = slice(hh * LANES, (hh + 1) * LANES)
            out_ref[:, sl] = _rope_block(y[:, sl], c, s_up, s_dn, ROPE_DIM // 2).astype(BF16)
    av_ref[...] = proj(2).astype(BF16)
    az_ref[...] = _silu(proj(3)).astype(BF16)

    u = proj(4) * proj(6)

    @pl.when(i % tiles_per_seq == 0)
    def _():
        ubuf_ref[0:8, :] = jnp.zeros((8, GROUP), F32)

    ubuf_ref[8:8 + tm, :] = u
    conv = (cw_ref[0:1, :] * ubuf_ref[6:6 + tm, :] + cw_ref[1:2, :] * ubuf_ref[7:7 + tm, :]
            + cw_ref[2:3, :] * u)
    ubuf_ref[0:8, :] = u[tm - 8:tm, :]
    c_ref[...] = (proj(5) * conv * _silu(proj(7))).astype(BF16)

    y = proj(8)
    c = rtab_ref[:, 0:LANES]
    s_up = rtab_ref[:, LANES:2 * LANES]
    s_dn = rtab_ref[:, 2 * LANES:3 * LANES]
    lane = lax.broadcasted_iota(jnp.int32, (tm, LANES), 1)
    low = lane < R_QK_DIM
    for hh in range(R_HEADS):
        sl = slice(hh * LANES, (hh + 1) * LANES)
        rot = _rope_block(y[:, sl], c, s_up, s_dn, R_QK_DIM // 2)
        rq_ref[:, sl] = jnp.where(low, rot, 0.0).astype(BF16)
        rk_ref[:, sl] = jnp.where(low, pltpu.roll(rot, R_QK_DIM, 1), 0.0).astype(BF16)
    rv_ref[...] = proj(9).astype(BF16)
    rz_ref[...] = _silu(proj(10)).astype(BF16)


def _inproj(x2, norm_g, w1, atab, rtab, conv_w, seq):
    n, d = x2.shape
    tm = PROJ_ROWS
    tiles_per_seq = seq // tm
    row = lambda i: (i, 0)
    pos = lambda i: (i % tiles_per_seq, 0)
    const = lambda i: (0, 0)
    out_sds = jax.ShapeDtypeStruct((n, GROUP), BF16)
    out_spec = pl.BlockSpec((tm, GROUP), row)
    return pl.pallas_call(
        functools.partial(_inproj_kernel, tiles_per_seq=tiles_per_seq),
        grid=(n // tm,),
        in_specs=[
            pl.BlockSpec((tm, d), row),
            pl.BlockSpec((1, d), const),
            pl.BlockSpec(w1.shape, const),
            pl.BlockSpec((tm, atab.shape[1]), pos),
            pl.BlockSpec((tm, rtab.shape[1]), pos),
            pl.BlockSpec(conv_w.shape, const),
        ],
        out_specs=[out_spec] * 9,
        out_shape=[out_sds] * 9,
        scratch_shapes=[pltpu.VMEM((tm + 8, GROUP), F32)],
        compiler_params=pltpu.CompilerParams(
            dimension_semantics=("arbitrary",), vmem_limit_bytes=VMEM_LIMIT),
        name="inproj",
    )(x2, norm_g, w1, atab, rtab, conv_w)


def _attn_kernel(lam_ref, g_ref, q_ref, k_ref, v_ref, z_ref, o_ref,
                 qs_ref, m_ref, l_ref, acc_ref, *, lam_init):
    i = pl.program_id(2)
    tq = q_ref.shape[0]
    q = q_ref[...]
    lane = lax.broadcasted_iota(jnp.int32, q.shape, 1)
    zero = jnp.zeros_like(q)
    qs_ref[0:tq, :] = jnp.where(lane < A_QK_DIM, q, zero)
    qs_ref[tq:2 * tq, :] = jnp.where(lane >= A_QK_DIM, q, zero)
    m_ref[...] = jnp.full(m_ref.shape, NEG_INF, F32)
    l_ref[...] = jnp.zeros(l_ref.shape, F32)
    acc_ref[...] = jnp.zeros(acc_ref.shape, F32)

    wide = 2 * tq

    def scores(kk):
        rows = pl.ds(pl.multiple_of(kk * wide, wide), wide)
        return lax.dot_general(qs_ref[...], k_ref[rows, :], (((1,), (1,)), ((), ())),
                               preferred_element_type=F32)

    def update(kk, s):
        width = s.shape[1]
        rows = pl.ds(pl.multiple_of(kk * wide, wide), width)
        m_prev = m_ref[...]
        m_new = jnp.maximum(m_prev, jnp.max(s, axis=1, keepdims=True))
        alpha = jnp.exp2(m_prev - m_new)
        m_ref[...] = m_new
        p = jnp.concatenate(
            [jnp.exp2(s[:, c * LANES:(c + 1) * LANES] - m_new).astype(BF16)
             for c in range(width // LANES)], axis=1)
        vs = v_ref[rows, :]
        v1 = jnp.concatenate([vs, jnp.ones_like(vs)], axis=1)
        for comp in range(2):
            rsl = slice(comp * tq, (comp + 1) * tq)
            res = jnp.dot(p[rsl], v1, preferred_element_type=F32)
            acc_ref[rsl, :] = alpha[rsl] * acc_ref[rsl, :] + res[:, :LANES]
            l_ref[rsl, :] = alpha[rsl] * l_ref[rsl, :] + res[:, LANES:]

    nfull = i // 2

    def body(kk, carry):
        update(kk, scores(kk))
        return carry

    lax.fori_loop(0, nfull, body, 0)

    def diagonal(width):
        rows = pl.ds(pl.multiple_of(nfull * wide, wide), width)
        s = lax.dot_general(qs_ref[...], k_ref[rows, :], (((1,), (1,)), ((), ())),
                            preferred_element_type=F32)
        r = lax.broadcasted_iota(jnp.int32, s.shape, 0)
        cidx = lax.broadcasted_iota(jnp.int32, s.shape, 1)
        visible = cidx + (nfull * wide - i * tq) <= jnp.where(r >= tq, r - tq, r)
        update(nfull, jnp.where(visible, s, NEG_INF))

    @pl.when(i % 2 == 1)
    def _():
        diagonal(wide)

    @pl.when(i % 2 == 0)
    def _():
        diagonal(tq)

    o = acc_ref[...] / l_ref[...]
    al = lam_ref[...]
    lam = (jnp.exp(jnp.sum(al[0:1] * al[1:2], axis=1, keepdims=True))
           - jnp.exp(jnp.sum(al[2:3] * al[3:4], axis=1, keepdims=True)) + lam_init)
    d = o[:tq] - lam * o[tq:]
    d = _rms(d) * g_ref[...] * (1.0 - lam_init)
    o_ref[...] = (d * z_ref[...].astype(F32)).astype(BF16)


def _attention(aq, ak, av, az, attn_lambda, subln_g, lam_init):
    b, s, _ = aq.shape
    tq = ATT_TILE
    tile = pl.BlockSpec((None, tq, LANES), lambda bi, hi, qi: (bi, qi, hi))
    full = pl.BlockSpec((None, s, LANES), lambda bi, hi, qi: (bi, 0, hi))
    const = lambda bi, hi, qi: (0, 0)
    return pl.pallas_call(
        functools.partial(_attn_kernel, lam_init=lam_init),
        grid=(b, A_HEADS, s // tq),
        in_specs=[
            pl.BlockSpec(attn_lambda.shape, const),
            pl.BlockSpec(subln_g.shape, const),
            tile, full, full, tile,
        ],
        out_specs=tile,
        out_shape=jax.ShapeDtypeStruct(aq.shape, BF16),
        scratch_shapes=[pltpu.VMEM((2 * tq, LANES), BF16),
                        pltpu.VMEM((2 * tq, LANES), F32), pltpu.VMEM((2 * tq, LANES), F32),
                        pltpu.VMEM((2 * tq, LANES), F32)],
        compiler_params=pltpu.CompilerParams(
            dimension_semantics=("arbitrary", "arbitrary", "arbitrary"),
            vmem_limit_bytes=VMEM_LIMIT),
        name="diff_attention",
    )(attn_lambda, subln_g, aq, ak, av, az)


def _ret_kernel(q_ref, k_ref, v_ref, z_ref, dm_ref, xi_ref, zeta_ref, dec_ref, o_ref, st_ref):
    @pl.when(pl.program_id(1) == 0)
    def _():
        st_ref[...] = jnp.zeros(st_ref.shape, F32)

    for hh in range(R_HEADS):
        sl = slice(hh * LANES, (hh + 1) * LANES)
        q = q_ref[:, sl]
        k = k_ref[:, sl]
        v = v_ref[:, sl]
        s = lax.dot_general(q, k, (((1,), (1,)), ((), ())), preferred_element_type=F32)
        inner = jnp.dot((s * dm_ref[hh]).astype(BF16), v, preferred_element_type=F32)
        st = st_ref[hh]
        cross = jnp.dot(q, st.astype(BF16), preferred_element_type=F32) * xi_ref[hh]
        o = _rms(inner + cross)
        o_ref[:, sl] = (o * z_ref[:, sl].astype(F32)).astype(BF16)
        kz = (k.astype(F32) * zeta_ref[hh]).astype(BF16)
        kv = lax.dot_general(kz, v, (((0,), (0,)), ((), ())), preferred_element_type=F32)
        st_ref[hh] = dec_ref[hh] * st + kv


def _retention(rq, rk, rv, rz, dmask, xi, zeta, dec):
    b, s, w = rq.shape
    c = RET_CHUNK
    tile = pl.BlockSpec((None, c, w), lambda bi, ti: (bi, ti, 0))
    const3 = lambda bi, ti: (0, 0, 0)
    return pl.pallas_call(
        _ret_kernel,
        grid=(b, s // c),
        in_specs=[tile, tile, tile, tile,
                  pl.BlockSpec(dmask.shape, const3), pl.BlockSpec(xi.shape, const3),
                  pl.BlockSpec(zeta.shape, const3), pl.BlockSpec(dec.shape, const3)],
        out_specs=tile,
        out_shape=jax.ShapeDtypeStruct(rq.shape, BF16),
        scratch_shapes=[pltpu.VMEM((R_HEADS, LANES, LANES), F32)],
        compiler_params=pltpu.CompilerParams(
            dimension_semantics=("arbitrary", "arbitrary"), vmem_limit_bytes=VMEM_LIMIT),
        name="retention",
    )(rq, rk, rv, rz, dmask, xi, zeta, dec)


def _merge_kernel(x_ref, a_ref, c_ref, r_ref, g_ref, wg_ref, wb_ref, wo_ref, fg_ref, o_ref,
                  *, final_norm):
    x = x_ref[...]
    d = x.shape[1]
    h = (_rms(x) * g_ref[...]).astype(BF16)
    merged = None
    for bi, br_ref in enumerate((a_ref, c_ref, r_ref)):
        t = jnp.dot(h, wg_ref[:, bi * d:(bi + 1) * d], preferred_element_type=F32)
        gate = 1.0 / (1.0 + jnp.exp(-t))
        contrib = gate * jnp.dot(br_ref[...], wb_ref[bi], preferred_element_type=F32)
        merged = contrib if merged is None else merged + contrib
    y = x + jnp.dot(merged.astype(BF16), wo_ref[...], preferred_element_type=F32)
    if final_norm:
        y = _rms(y) * fg_ref[...]
    o_ref[...] = y


def _merge(x2, a, c, r, norm_g, wg, wb, wo, final_g, final_norm):
    n, d = x2.shape
    tm = PROJ_ROWS
    row = lambda i: (i, 0)
    const = lambda i: (0, 0)
    br = pl.BlockSpec((tm, GROUP), row)
    return pl.pallas_call(
        functools.partial(_merge_kernel, final_norm=final_norm),
        grid=(n // tm,),
        in_specs=[pl.BlockSpec((tm, d), row), br, br, br,
                  pl.BlockSpec((1, d), const),
                  pl.BlockSpec(wg.shape, const),
                  pl.BlockSpec(wb.shape, lambda i: (0, 0, 0)),
                  pl.BlockSpec(wo.shape, const),
                  pl.BlockSpec((1, d), const)],
        out_specs=pl.BlockSpec((tm, d), row),
        out_shape=jax.ShapeDtypeStruct((n, d), F32),
        compiler_params=pltpu.CompilerParams(
            dimension_semantics=("arbitrary",), vmem_limit_bytes=VMEM_LIMIT),
        name="merge_outproj",
    )(x2, a, c, r, norm_g, wg, wb, wo, final_g)


def _attn_rope_table(seq):
    pos = jnp.arange(seq, dtype=F32)
    inv = ROPE_THETA ** (-jnp.arange(0, ROPE_DIM, 2, dtype=F32) / ROPE_DIM)
    ang = pos[:, None] * inv[None, :]
    cos, sin = jnp.cos(ang), jnp.sin(ang)
    half = ROPE_DIM // 2
    dd = jnp.arange(LANES) % A_QK_DIM
    j = dd % half
    c = jnp.where(dd[None, :] < ROPE_DIM, cos[:, j], 1.0)
    s_up = jnp.where(dd[None, :] < half, -sin[:, j], 0.0)
    s_dn = jnp.where((dd[None, :] >= half) & (dd[None, :] < ROPE_DIM), sin[:, j], 0.0)
    k_tab = jnp.concatenate([c, s_up, s_dn], axis=1)
    q_scale = (A_QK_DIM ** -0.5) * math.log2(math.e)
    return jnp.concatenate([k_tab * q_scale, k_tab], axis=1).astype(F32)


def _ret_rope_table(seq):
    pos = jnp.arange(seq, dtype=F32)
    inv = 1.0 / (RET_THETA ** jnp.linspace(0.0, 1.0, R_QK_DIM // 2, dtype=F32))
    ang = pos[:, None] * inv[None, :]
    cos, sin = jnp.cos(ang), jnp.sin(ang)
    half = R_QK_DIM // 2
    dd = jnp.arange(LANES) % R_QK_DIM
    j = dd % half
    c = cos[:, j]
    s_up = jnp.where(dd[None, :] < half, -sin[:, j], 0.0)
    s_dn = jnp.where(dd[None, :] >= half, sin[:, j], 0.0)
    scale = jnp.where(jnp.arange(LANES) < R_QK_DIM, 1.0, R_QK_DIM ** -0.5)[None, :]
    return jnp.concatenate([c * scale, s_up * scale, s_dn * scale], axis=1).astype(F32)


def _ret_decay_tables(chunk):
    log_g = jnp.log(1.0 - 2.0 ** (-5.0 - jnp.arange(R_HEADS, dtype=F32)))
    idx = jnp.arange(chunk, dtype=F32)
    diff = idx[:, None] - idx[None, :]
    dmask = jnp.where(diff >= 0,
                      jnp.exp(jnp.where(diff >= 0, diff, 0.0)[None] * log_g[:, None, None]),
                      0.0)
    ones = jnp.ones((1, 1, LANES), F32)
    xi = jnp.exp((idx + 1.0)[None, :] * log_g[:, None])[:, :, None] * ones
    zeta = jnp.exp((chunk - 1 - idx)[None, :] * log_g[:, None])[:, :, None] * ones
    dec = jnp.exp(chunk * log_g)[:, None, None] * ones
    return dmask.astype(F32), xi.astype(F32), zeta.astype(F32), dec.astype(F32)


def _split_w_in(w_in_l):
    d = w_in_l.shape[0]
    g = GROUP
    mix = w_in_l[:, :8 * g]
    rq = w_in_l[:, 8 * g:8 * g + g // 2].reshape(d, R_HEADS, R_QK_DIM)
    rk = w_in_l[:, 8 * g + g // 2:9 * g].reshape(d, R_HEADS, R_QK_DIM)
    rqk = jnp.concatenate([rq, rk], axis=2).reshape(d, g)
    rest = w_in_l[:, 9 * g:11 * g]
    w1 = jnp.concatenate([mix, rqk, rest], axis=1).astype(BF16)
    wg = w_in_l[:, 11 * g:].astype(BF16)
    return w1, wg


def kernel(x, norm_g, w_in, attn_lambda, attn_subln_g, conv_w, w_branch, w_out, final_norm_g):
    b, s, d = x.shape
    depth = w_in.shape[0]
    x2 = x.reshape(b * s, d)
    atab = _attn_rope_table(s)
    rtab = _ret_rope_table(s)
    dmask, xi, zeta, dec = _ret_decay_tables(RET_CHUNK)
    final_g = final_norm_g.reshape(1, d)
    for layer in range(depth):
        lam_init = 0.8 - 0.6 * math.exp(-0.3 * layer)
        w1, wg = _split_w_in(w_in[layer])
        g = norm_g[layer].reshape(1, d)
        aq, ak, av, az, c, rq, rk, rv, rz = _inproj(x2, g, w1, atab, rtab, conv_w[layer], s)
        sh = (b, s, GROUP)
        a = _attention(aq.reshape(sh), ak.reshape(sh), av.reshape(sh), az.reshape(sh),
                       attn_lambda[layer], attn_subln_g[layer].reshape(1, LANES), lam_init)
        r = _retention(rq.reshape(sh), rk.reshape(sh), rv.reshape(sh), rz.reshape(sh),
                       dmask, xi, zeta, dec)
        x2 = _merge(x2, a.reshape(b * s, GROUP), c, r.reshape(b * s, GROUP), g, wg,
                    w_branch[layer].astype(BF16), w_out[layer].astype(BF16), final_g,
                    layer == depth - 1)
    return x2.reshape(b, s, d)
```

```python
import functools
import math

import jax
import jax.numpy as jnp
from jax import lax
from jax.experimental import pallas as pl
from jax.experimental.pallas import tpu as pltpu

F32 = jnp.float32
BF16 = jnp.bfloat16

A_HEADS = 4
A_QK_DIM = 64
ROPE_THETA = 500000.0
ROPE_DIM = A_QK_DIM // 4
NEG_INF = -1e30
CONV_WIDTH = 3
R_HEADS = 4
R_QK_DIM = 64
RET_THETA = 10000.0
N_BRANCH = 3
EPS = 1e-6

LANES = 128
GROUP = 512
VMEM_LIMIT = 56 * 1024 * 1024

PROJ_ROWS = 512
ATT_TILE = 512
RET_CHUNK = 512


def _silu(t):
    return t * (1.0 / (1.0 + jnp.exp(-t)))


def _rms(x):
    return x * lax.rsqrt(jnp.mean(x * x, axis=-1, keepdims=True) + EPS)


def _rope_block(blk, c, s_up, s_dn, shift):
    return (blk * c + pltpu.roll(blk, LANES - shift, 1) * s_up
            + pltpu.roll(blk, shift, 1) * s_dn)


def _inproj_kernel(x_ref, g_ref, w_ref, atab_ref, rtab_ref, cw_ref,
                   aq_ref, ak_ref, av_ref, az_ref, c_ref,
                   rq_ref, rk_ref, rv_ref, rz_ref, ubuf_ref, *, tiles_per_seq):
    i = pl.program_id(0)
    tm = x_ref.shape[0]
    h = (_rms(x_ref[...]) * g_ref[...]).astype(BF16)

    def proj(gi):
        return jnp.dot(h, w_ref[:, gi * GROUP:(gi + 1) * GROUP], preferred_element_type=F32)

    for gi, out_ref, t0 in ((0, aq_ref, 0), (1, ak_ref, 3 * LANES)):
        y = proj(gi)
        c = atab_ref[:, t0:t0 + LANES]
        s_up = atab_ref[:, t0 + LANES:t0 + 2 * LANES]
        s_dn = atab_ref[:, t0 + 2 * LANES:t0 + 3 * LANES]
        for hh in range(A_HEADS):
            sl = slice(hh * LANES, (hh + 1) * LANES)
            out_ref[:, sl] = _rope_block(y[:, sl], c, s_up, s_dn, ROPE_DIM // 2).astype(BF16)
    av_ref[...] = proj(2).astype(BF16)
    az_ref[...] = _silu(proj(3)).astype(BF16)

    u = proj(4) * proj(6)

    @pl.when(i % tiles_per_seq == 0)
    def _():
        ubuf_ref[0:8, :] = jnp.zeros((8, GROUP), F32)

    ubuf_ref[8:8 + tm, :] = u
    conv = (cw_ref[0:1, :] * ubuf_ref[6:6 + tm, :] + cw_ref[1:2, :] * ubuf_ref[7:7 + tm, :]
            + cw_ref[2:3, :] * u)
    ubuf_ref[0:8, :] = u[tm - 8:tm, :]
    c_ref[...] = (proj(5) * conv * _silu(proj(7))).astype(BF16)

    y = proj(8)
    c = rtab_ref[:, 0:LANES]
    s_up = rtab_ref[:, LANES:2 * LANES]
    s_dn = rtab_ref[:, 2 * LANES:3 * LANES]
    lane = lax.broadcasted_iota(jnp.int32, (tm, LANES), 1)
    low = lane < R_QK_DIM
    for hh in range(R_HEADS):
        sl = slice(hh * LANES, (hh + 1) * LANES)
        rot = _rope_block(y[:, sl], c, s_up, s_dn, R_QK_DIM // 2)
        rq_ref[:, sl] = jnp.where(low, rot, 0.0).astype(BF16)
        rk_ref[:, sl] = jnp.where(low, pltpu.roll(rot, R_QK_DIM, 1), 0.0).astype(BF16)
    rv_ref[...] = proj(9).astype(BF16)
    rz_ref[...] = _silu(proj(10)).astype(BF16)


def _inproj(x2, norm_g, w1, atab, rtab, conv_w, seq):
    n, d = x2.shape
    tm = PROJ_ROWS
    tiles_per_seq = seq // tm
    row = lambda i: (i, 0)
    pos = lambda i: (i % tiles_per_seq, 0)
    const = lambda i: (0, 0)
    out_sds = jax.ShapeDtypeStruct((n, GROUP), BF16)
    out_spec = pl.BlockSpec((tm, GROUP), row)
    return pl.pallas_call(
        functools.partial(_inproj_kernel, tiles_per_seq=tiles_per_seq),
        grid=(n // tm,),
        in_specs=[
            pl.BlockSpec((tm, d), row),
            pl.BlockSpec((1, d), const),
            pl.BlockSpec(w1.shape, const),
            pl.BlockSpec((tm, atab.shape[1]), pos),
            pl.BlockSpec((tm, rtab.shape[1]), pos),
            pl.BlockSpec(conv_w.shape, const),
        ],
        out_specs=[out_spec] * 9,
        out_shape=[out_sds] * 9,
        scratch_shapes=[pltpu.VMEM((tm + 8, GROUP), F32)],
        compiler_params=pltpu.CompilerParams(
            dimension_semantics=("arbitrary",), vmem_limit_bytes=VMEM_LIMIT),
        name="inproj",
    )(x2, norm_g, w1, atab, rtab, conv_w)


def _attn_kernel(lam_ref, g_ref, q_ref, k_ref, v_ref, z_ref, o_ref,
                 qs_ref, m_ref, l_ref, acc_ref, *, lam_init):
    i = pl.program_id(2)
    tq = q_ref.shape[0]
    q = q_ref[...]
    lane = lax.broadcasted_iota(jnp.int32, q.shape, 1)
    zero = jnp.zeros_like(q)
    qs_ref[0:tq, :] = jnp.where(lane < A_QK_DIM, q, zero)
    qs_ref[tq:2 * tq, :] = jnp.where(lane >= A_QK_DIM, q, zero)
    m_ref[...] = jnp.full(m_ref.shape, NEG_INF, F32)
    l_ref[...] = jnp.zeros(l_ref.shape, F32)
    acc_ref[...] = jnp.zeros(acc_ref.shape, F32)

    wide = 2 * tq

    def scores(kk):
        rows = pl.ds(pl.multiple_of(kk * wide, wide), wide)
        return lax.dot_general(qs_ref[...], k_ref[rows, :], (((1,), (1,)), ((), ())),
                               preferred_element_type=F32)

    def update(kk, s):
        width = s.shape[1]
        rows = pl.ds(pl.multiple_of(kk * wide, wide), width)
        m_prev = m_ref[...]
        m_new = jnp.maximum(m_prev, jnp.max(s, axis=1, keepdims=True))
        alpha = jnp.exp2(m_prev - m_new)
        m_ref[...] = m_new
        psum = None
        pcs = []
        for c in range(width // LANES):
            pc = jnp.exp2(s[:, c * LANES:(c + 1) * LANES] - m_new)
            psum = pc if psum is None else psum + pc
            pcs.append(pc.astype(BF16))
        l_ref[...] = alpha * l_ref[...] + psum
        acc_ref[...] = alpha * acc_ref[...] + jnp.dot(
            jnp.concatenate(pcs, axis=1), v_ref[rows, :], preferred_element_type=F32)

    nfull = i // 2

    def body(kk, carry):
        update(kk, scores(kk))
        return carry

    lax.fori_loop(0, nfull, body, 0)

    def diagonal(width):
        rows = pl.ds(pl.multiple_of(nfull * wide, wide), width)
        s = lax.dot_general(qs_ref[...], k_ref[rows, :], (((1,), (1,)), ((), ())),
                            preferred_element_type=F32)
        r = lax.broadcasted_iota(jnp.int32, s.shape, 0)
        cidx = lax.broadcasted_iota(jnp.int32, s.shape, 1)
        visible = cidx + (nfull * wide - i * tq) <= jnp.where(r >= tq, r - tq, r)
        update(nfull, jnp.where(visible, s, NEG_INF))

    @pl.when(i % 2 == 1)
    def _():
        diagonal(wide)

    @pl.when(i % 2 == 0)
    def _():
        diagonal(tq)

    o = acc_ref[...] / jnp.sum(l_ref[...], axis=1, keepdims=True)
    al = lam_ref[...]
    lam = (jnp.exp(jnp.sum(al[0:1] * al[1:2], axis=1, keepdims=True))
           - jnp.exp(jnp.sum(al[2:3] * al[3:4], axis=1, keepdims=True)) + lam_init)
    d = o[:tq] - lam * o[tq:]
    d = _rms(d) * g_ref[...] * (1.0 - lam_init)
    o_ref[...] = (d * z_ref[...].astype(F32)).astype(BF16)


def _attention(aq, ak, av, az, attn_lambda, subln_g, lam_init):
    b, s, _ = aq.shape
    tq = ATT_TILE
    tile = pl.BlockSpec((None, tq, LANES), lambda bi, hi, qi: (bi, qi, hi))
    full = pl.BlockSpec((None, s, LANES), lambda bi, hi, qi: (bi, 0, hi))
    const = lambda bi, hi, qi: (0, 0)
    return pl.pallas_call(
        functools.partial(_attn_kernel, lam_init=lam_init),
        grid=(b, A_HEADS, s // tq),
        in_specs=[
            pl.BlockSpec(attn_lambda.shape, const),
            pl.BlockSpec(subln_g.shape, const),
            tile, full, full, tile,
        ],
        out_specs=tile,
        out_shape=jax.ShapeDtypeStruct(aq.shape, BF16),
        scratch_shapes=[pltpu.VMEM((2 * tq, LANES), BF16),
                        pltpu.VMEM((2 * tq, LANES), F32), pltpu.VMEM((2 * tq, LANES), F32),
                        pltpu.VMEM((2 * tq, LANES), F32)],
        compiler_params=pltpu.CompilerParams(
            dimension_semantics=("arbitrary", "arbitrary", "arbitrary"),
            vmem_limit_bytes=VMEM_LIMIT),
        name="diff_attention",
    )(attn_lambda, subln_g, aq, ak, av, az)


def _ret_kernel(q_ref, k_ref, v_ref, z_ref, dm_ref, xi_ref, zeta_ref, dec_ref, o_ref, st_ref):
    @pl.when(pl.program_id(1) == 0)
    def _():
        st_ref[...] = jnp.zeros(st_ref.shape, F32)

    for hh in range(R_HEADS):
        sl = slice(hh * LANES, (hh + 1) * LANES)
        q = q_ref[:, sl]
        k = k_ref[:, sl]
        v = v_ref[:, sl]
        s = lax.dot_general(q, k, (((1,), (1,)), ((), ())), preferred_element_type=F32)
        inner = jnp.dot((s * dm_ref[hh]).astype(BF16), v, preferred_element_type=F32)
        st = st_ref[hh]
        cross = jnp.dot(q, st.astype(BF16), preferred_element_type=F32) * xi_ref[hh]
        o = _rms(inner + cross)
        o_ref[:, sl] = (o * z_ref[:, sl].astype(F32)).astype(BF16)
        kz = (k.astype(F32) * zeta_ref[hh]).astype(BF16)
        kv = lax.dot_general(kz, v, (((0,), (0,)), ((), ())), preferred_element_type=F32)
        st_ref[hh] = dec_ref[hh] * st + kv


def _retention(rq, rk, rv, rz, dmask, xi, zeta, dec):
    b, s, w = rq.shape
    c = RET_CHUNK
    tile = pl.BlockSpec((None, c, w), lambda bi, ti: (bi, ti, 0))
    const3 = lambda bi, ti: (0, 0, 0)
    return pl.pallas_call(
        _ret_kernel,
        grid=(b, s // c),
        in_specs=[tile, tile, tile, tile,
                  pl.BlockSpec(dmask.shape, const3), pl.BlockSpec(xi.shape, const3),
                  pl.BlockSpec(zeta.shape, const3), pl.BlockSpec(dec.shape, const3)],
        out_specs=tile,
        out_shape=jax.ShapeDtypeStruct(rq.shape, BF16),
        scratch_shapes=[pltpu.VMEM((R_HEADS, LANES, LANES), F32)],
        compiler_params=pltpu.CompilerParams(
            dimension_semantics=("arbitrary", "arbitrary"), vmem_limit_bytes=VMEM_LIMIT),
        name="retention",
    )(rq, rk, rv, rz, dmask, xi, zeta, dec)


def _merge_kernel(x_ref, a_ref, c_ref, r_ref, g_ref, wg_ref, wb_ref, wo_ref, fg_ref, o_ref,
                  *, final_norm):
    x = x_ref[...]
    d = x.shape[1]
    h = (_rms(x) * g_ref[...]).astype(BF16)
    merged = None
    for bi, br_ref in enumerate((a_ref, c_ref, r_ref)):
        t = jnp.dot(h, wg_ref[:, bi * d:(bi + 1) * d], preferred_element_type=F32)
        gate = 1.0 / (1.0 + jnp.exp(-t))
        contrib = gate * jnp.dot(br_ref[...], wb_ref[bi], preferred_element_type=F32)
        merged = contrib if merged is None else merged + contrib
    y = x + jnp.dot(merged.astype(BF16), wo_ref[...], preferred_element_type=F32)
    if final_norm:
        y = _rms(y) * fg_ref[...]
    o_ref[...] = y


def _merge(x2, a, c, r, norm_g, wg, wb, wo, final_g, final_norm):
    n, d = x2.shape
    tm = PROJ_ROWS
    row = lambda i: (i, 0)
    const = lambda i: (0, 0)
    br = pl.BlockSpec((tm, GROUP), row)
    return pl.pallas_call(
        functools.partial(_merge_kernel, final_norm=final_norm),
        grid=(n // tm,),
        in_specs=[pl.BlockSpec((tm, d), row), br, br, br,
                  pl.BlockSpec((1, d), const),
                  pl.BlockSpec(wg.shape, const),
                  pl.BlockSpec(wb.shape, lambda i: (0, 0, 0)),
                  pl.BlockSpec(wo.shape, const),
                  pl.BlockSpec((1, d), const)],
        out_specs=pl.BlockSpec((tm, d), row),
        out_shape=jax.ShapeDtypeStruct((n, d), F32),
        compiler_params=pltpu.CompilerParams(
            dimension_semantics=("arbitrary",), vmem_limit_bytes=VMEM_LIMIT),
        name="merge_outproj",
    )(x2, a, c, r, norm_g, wg, wb, wo, final_g)


def _attn_rope_table(seq):
    pos = jnp.arange(seq, dtype=F32)
    inv = ROPE_THETA ** (-jnp.arange(0, ROPE_DIM, 2, dtype=F32) / ROPE_DIM)
    ang = pos[:, None] * inv[None, :]
    half = ROPE_DIM // 2
    cos = jnp.tile(jnp.cos(ang), (1, LANES // half))
    sin = jnp.tile(jnp.sin(ang), (1, LANES // half))
    dd = (jnp.arange(LANES) % A_QK_DIM)[None, :]
    c = jnp.where(dd < ROPE_DIM, cos, 1.0)
    s_up = jnp.where(dd < half, -sin, 0.0)
    s_dn = jnp.where((dd >= half) & (dd < ROPE_DIM), sin, 0.0)
    k_tab = jnp.concatenate([c, s_up, s_dn], axis=1)
    q_scale = (A_QK_DIM ** -0.5) * math.log2(math.e)
    return jnp.concatenate([k_tab * q_scale, k_tab], axis=1).astype(F32)


def _ret_rope_table(seq):
    pos = jnp.arange(seq, dtype=F32)
    inv = 1.0 / (RET_THETA ** jnp.linspace(0.0, 1.0, R_QK_DIM // 2, dtype=F32))
    ang = pos[:, None] * inv[None, :]
    half = R_QK_DIM // 2
    c = jnp.tile(jnp.cos(ang), (1, LANES // half))
    sin = jnp.tile(jnp.sin(ang), (1, LANES // half))
    dd = (jnp.arange(LANES) % R_QK_DIM)[None, :]
    s_up = jnp.where(dd < half, -sin, 0.0)
    s_dn = jnp.where(dd >= half, sin, 0.0)
    scale = jnp.where(jnp.arange(LANES) < R_QK_DIM, 1.0, R_QK_DIM ** -0.5)[None, :]
    return jnp.concatenate([c * scale, s_up * scale, s_dn * scale], axis=1).astype(F32)


def _ret_decay_tables(chunk):
    log_g = jnp.log(1.0 - 2.0 ** (-5.0 - jnp.arange(R_HEADS, dtype=F32)))
    idx = jnp.arange(chunk, dtype=F32)
    diff = idx[:, None] - idx[None, :]
    dmask = jnp.where(diff >= 0,
                      jnp.exp(jnp.where(diff >= 0, diff, 0.0)[None] * log_g[:, None, None]),
                      0.0)
    ones = jnp.ones((1, 1, LANES), F32)
    xi = jnp.exp((idx + 1.0)[None, :] * log_g[:, None])[:, :, None] * ones
    zeta = jnp.exp((chunk - 1 - idx)[None, :] * log_g[:, None])[:, :, None] * ones
    dec = jnp.exp(chunk * log_g)[:, None, None] * ones
    return dmask.astype(F32), xi.astype(F32), zeta.astype(F32), dec.astype(F32)


def _split_w_in(w_in_l):
    d = w_in_l.shape[0]
    g = GROUP
    mix = w_in_l[:, :8 * g]
    rq = w_in_l[:, 8 * g:8 * g + g // 2].reshape(d, R_HEADS, R_QK_DIM)
    rk = w_in_l[:, 8 * g + g // 2:9 * g].reshape(d, R_HEADS, R_QK_DIM)
    rqk = jnp.concatenate([rq, rk], axis=2).reshape(d, g)
    rest = w_in_l[:, 9 * g:11 * g]
    w1 = jnp.concatenate([mix, rqk, rest], axis=1).astype(BF16)
    wg = w_in_l[:, 11 * g:].astype(BF16)
    return w1, wg


def kernel(x, norm_g, w_in, attn_lambda, attn_subln_g, conv_w, w_branch, w_out, final_norm_g):
    b, s, d = x.shape
    depth = w_in.shape[0]
    x2 = x.reshape(b * s, d)
    atab = _attn_rope_table(s)
    rtab = _ret_rope_table(s)
    dmask, xi, zeta, dec = _ret_decay_tables(RET_CHUNK)
    final_g = final_norm_g.reshape(1, d)
    for layer in range(depth):
        lam_init = 0.8 - 0.6 * math.exp(-0.3 * layer)
        w1, wg = _split_w_in(w_in[layer])
        g = norm_g[layer].reshape(1, d)
        aq, ak, av, az, c, rq, rk, rv, rz = _inproj(x2, g, w1, atab, rtab, conv_w[layer], s)
        sh = (b, s, GROUP)
        a = _attention(aq.reshape(sh), ak.reshape(sh), av.reshape(sh), az.reshape(sh),
                       attn_lambda[layer], attn_subln_g[layer].reshape(1, LANES), lam_init)
        r = _retention(rq.reshape(sh), rk.reshape(sh), rv.reshape(sh), rz.reshape(sh),
                       dmask, xi, zeta, dec)
        x2 = _merge(x2, a.reshape(b * s, GROUP), c, r.reshape(b * s, GROUP), g, wg,
                    w_branch[layer].astype(BF16), w_out[layer].astype(BF16), final_g,
                    layer == depth - 1)
    return x2.reshape(b, s, d)
```

```python
import functools
import math

import jax
import jax.numpy as jnp
from jax import lax
from jax.experimental import pallas as pl
from jax.experimental.pallas import tpu as pltpu

F32 = jnp.float32
BF16 = jnp.bfloat16

A_HEADS = 4
A_QK_DIM = 64
ROPE_THETA = 500000.0
ROPE_DIM = A_QK_DIM // 4
NEG_INF = -1e30
CONV_WIDTH = 3
R_HEADS = 4
R_QK_DIM = 64
RET_THETA = 10000.0
N_BRANCH = 3
EPS = 1e-6

LANES = 128
GROUP = 512
VMEM_LIMIT = 56 * 1024 * 1024

PROJ_ROWS = 512
ATT_TILE = 512
RET_CHUNK = 512


def _silu(t):
    return t * (1.0 / (1.0 + jnp.exp(-t)))


def _rms(x):
    return x * lax.rsqrt(jnp.mean(x * x, axis=-1, keepdims=True) + EPS)


def _rope_block(blk, c, s_up, s_dn, shift):
    return (blk * c + pltpu.roll(blk, LANES - shift, 1) * s_up
            + pltpu.roll(blk, shift, 1) * s_dn)


def _inproj_kernel(x_ref, g_ref, w_ref, atab_ref, rtab_ref, cw_ref,
                   aq_ref, ak_ref, av_ref, az_ref, c_ref,
                   rq_ref, rk_ref, rv_ref, rz_ref, ubuf_ref, *, tiles_per_seq):
    i = pl.program_id(0)
    tm = x_ref.shape[0]
    h = (_rms(x_ref[...]) * g_ref[...]).astype(BF16)

    def proj(gi):
        return jnp.dot(h, w_ref[:, gi * GROUP:(gi + 1) * GROUP], preferred_element_type=F32)

    for gi, out_ref, t0 in ((0, aq_ref, 0), (1, ak_ref, 3 * LANES)):
        y = proj(gi)
        c = atab_ref[:, t0:t0 + LANES]
        s_up = atab_ref[:, t0 + LANES:t0 + 2 * LANES]
        s_dn = atab_ref[:, t0 + 2 * LANES:t0 + 3 * LANES]
        for hh in range(A_HEADS):
            sl = slice(hh * LANES, (hh + 1) * LANES)
            out_ref[:, sl] = _rope_block(y[:, sl], c, s_up, s_dn, ROPE_DIM // 2).astype(BF16)
    av_ref[...] = proj(2).astype(BF16)
    az_ref[...] = _silu(proj(3)).astype(BF16)

    u = proj(4) * proj(6)

    @pl.when(i % tiles_per_seq == 0)
    def _():
        ubuf_ref[0:8, :] = jnp.zeros((8, GROUP), F32)

    ubuf_ref[8:8 + tm, :] = u
    conv = (cw_ref[0:1, :] * ubuf_ref[6:6 + tm, :] + cw_ref[1:2, :] * ubuf_ref[7:7 + tm, :]
            + cw_ref[2:3, :] * u)
    ubuf_ref[0:8, :] = u[tm - 8:tm, :]
    c_ref[...] = (proj(5) * conv * _silu(proj(7))).astype(BF16)

    y = proj(8)
    c = rtab_ref[:, 0:LANES]
    s_up = rtab_ref[:, LANES:2 * LANES]
    s_dn = rtab_ref[:, 2 * LANES:3 * LANES]
    lane = lax.broadcasted_iota(jnp.int32, (tm, LANES), 1)
    low = lane < R_QK_DIM
    for hh in range(R_HEADS):
        sl = slice(hh * LANES, (hh + 1) * LANES)
        rot = _rope_block(y[:, sl], c, s_up, s_dn, R_QK_DIM // 2)
        rq_ref[:, sl] = jnp.where(low, rot, 0.0).astype(BF16)
        rk_ref[:, sl] = jnp.where(low, pltpu.roll(rot, R_QK_DIM, 1), 0.0).astype(BF16)
    rv_ref[...] = proj(9).astype(BF16)
    rz_ref[...] = _silu(proj(10)).astype(BF16)


def _inproj(x2, norm_g, w1, atab, rtab, conv_w, seq):
    n, d = x2.shape
    tm = PROJ_ROWS
    tiles_per_seq = seq // tm
    row = lambda i: (i, 0)
    pos = lambda i: (i % tiles_per_seq, 0)
    const = lambda i: (0, 0)
    out_sds = jax.ShapeDtypeStruct((n, GROUP), BF16)
    out_spec = pl.BlockSpec((tm, GROUP), row)
    return pl.pallas_call(
        functools.partial(_inproj_kernel, tiles_per_seq=tiles_per_seq),
        grid=(n // tm,),
        in_specs=[
            pl.BlockSpec((tm, d), row),
            pl.BlockSpec((1, d), const),
            pl.BlockSpec(w1.shape, const),
            pl.BlockSpec((tm, atab.shape[1]), pos),
            pl.BlockSpec((tm, rtab.shape[1]), pos),
            pl.BlockSpec(conv_w.shape, const),
        ],
        out_specs=[out_spec] * 9,
        out_shape=[out_sds] * 9,
        scratch_shapes=[pltpu.VMEM((tm + 8, GROUP), F32)],
        compiler_params=pltpu.CompilerParams(
            dimension_semantics=("arbitrary",), vmem_limit_bytes=VMEM_LIMIT),
        name="inproj",
    )(x2, norm_g, w1, atab, rtab, conv_w)


def _attn_kernel(lam_ref, g_ref, q_ref, k_ref, v_ref, z_ref, o_ref,
                 qs_ref, m_ref, l_ref, acc_ref, *, lam_init):
    i = pl.program_id(2)
    tq = q_ref.shape[0]
    q = q_ref[...]
    lane = lax.broadcasted_iota(jnp.int32, q.shape, 1)
    zero = jnp.zeros_like(q)
    qs_ref[0:tq, :] = jnp.where(lane < A_QK_DIM, q, zero)
    qs_ref[tq:2 * tq, :] = jnp.where(lane >= A_QK_DIM, q, zero)
    m_ref[...] = jnp.full(m_ref.shape, NEG_INF, F32)
    l_ref[...] = jnp.zeros(l_ref.shape, F32)
    acc_ref[...] = jnp.zeros(acc_ref.shape, F32)

    def step(j, width, masked=False):
        rows = pl.ds(pl.multiple_of(j * tq, tq), width)
        s = lax.dot_general(qs_ref[...], k_ref[rows, :], (((1,), (1,)), ((), ())),
                            preferred_element_type=F32)
        if masked:
            r = lax.broadcasted_iota(jnp.int32, s.shape, 0)
            cidx = lax.broadcasted_iota(jnp.int32, s.shape, 1)
            s = jnp.where(cidx <= jnp.where(r >= tq, r - tq, r), s, NEG_INF)
        m_prev = m_ref[...]
        m_new = jnp.maximum(m_prev, jnp.max(s, axis=1, keepdims=True))
        alpha = jnp.exp2(m_prev - m_new)
        m_ref[...] = m_new
        p = jnp.concatenate(
            [jnp.exp2(s[:, c * LANES:(c + 1) * LANES] - m_new).astype(BF16)
             for c in range(width // LANES)], axis=1)
        vs = v_ref[rows, :]
        v1 = jnp.concatenate([vs, jnp.ones_like(vs)], axis=1)
        for comp in range(2):
            rsl = slice(comp * tq, (comp + 1) * tq)
            res = jnp.dot(p[rsl], v1, preferred_element_type=F32)
            acc_ref[rsl, :] = alpha[rsl] * acc_ref[rsl, :] + res[:, :LANES]
            l_ref[rsl, :] = alpha[rsl] * l_ref[rsl, :] + res[:, LANES:]

    def wide(kk, carry):
        step(2 * kk, 2 * tq)
        return carry

    lax.fori_loop(0, i // 2, wide, 0)

    @pl.when(i % 2 == 1)
    def _():
        step(i - 1, tq)

    step(i, tq, masked=True)

    o = acc_ref[...] / l_ref[...]
    al = lam_ref[...]
    lam = (jnp.exp(jnp.sum(al[0:1] * al[1:2], axis=1, keepdims=True))
           - jnp.exp(jnp.sum(al[2:3] * al[3:4], axis=1, keepdims=True)) + lam_init)
    d = o[:tq] - lam * o[tq:]
    d = _rms(d) * g_ref[...] * (1.0 - lam_init)
    o_ref[...] = (d * z_ref[...].astype(F32)).astype(BF16)


def _attention(aq, ak, av, az, attn_lambda, subln_g, lam_init):
    b, s, _ = aq.shape
    tq = ATT_TILE
    tile = pl.BlockSpec((None, tq, LANES), lambda bi, hi, qi: (bi, qi, hi))
    full = pl.BlockSpec((None, s, LANES), lambda bi, hi, qi: (bi, 0, hi))
    const = lambda bi, hi, qi: (0, 0)
    return pl.pallas_call(
        functools.partial(_attn_kernel, lam_init=lam_init),
        grid=(b, A_HEADS, s // tq),
        in_specs=[
            pl.BlockSpec(attn_lambda.shape, const),
            pl.BlockSpec(subln_g.shape, const),
            tile, full, full, tile,
        ],
        out_specs=tile,
        out_shape=jax.ShapeDtypeStruct(aq.shape, BF16),
        scratch_shapes=[pltpu.VMEM((2 * tq, LANES), BF16),
                        pltpu.VMEM((2 * tq, LANES), F32), pltpu.VMEM((2 * tq, LANES), F32),
                        pltpu.VMEM((2 * tq, LANES), F32)],
        compiler_params=pltpu.CompilerParams(
            dimension_semantics=("arbitrary", "arbitrary", "arbitrary"),
            vmem_limit_bytes=VMEM_LIMIT),
        name="diff_attention",
    )(attn_lambda, subln_g, aq, ak, av, az)


def _ret_kernel(q_ref, k_ref, v_ref, z_ref, dm_ref, xi_ref, zeta_ref, dec_ref, o_ref, st_ref):
    @pl.when(pl.program_id(1) == 0)
    def _():
        st_ref[...] = jnp.zeros(st_ref.shape, F32)

    for hh in range(R_HEADS):
        sl = slice(hh * LANES, (hh + 1) * LANES)
        q = q_ref[:, sl]
        k = k_ref[:, sl]
        v = v_ref[:, sl]
        s = lax.dot_general(q, k, (((1,), (1,)), ((), ())), preferred_element_type=F32)
        inner = jnp.dot((s * dm_ref[hh]).astype(BF16), v, preferred_element_type=F32)
        st = st_ref[hh]
        cross = jnp.dot(q, st.astype(BF16), preferred_element_type=F32) * xi_ref[hh]
        o = _rms(inner + cross)
        o_ref[:, sl] = (o * z_ref[:, sl].astype(F32)).astype(BF16)
        kz = (k.astype(F32) * zeta_ref[hh]).astype(BF16)
        kv = lax.dot_general(kz, v, (((0,), (0,)), ((), ())), preferred_element_type=F32)
        st_ref[hh] = dec_ref[hh] * st + kv


def _retention(rq, rk, rv, rz, dmask, xi, zeta, dec):
    b, s, w = rq.shape
    c = RET_CHUNK
    tile = pl.BlockSpec((None, c, w), lambda bi, ti: (bi, ti, 0))
    const3 = lambda bi, ti: (0, 0, 0)
    return pl.pallas_call(
        _ret_kernel,
        grid=(b, s // c),
        in_specs=[tile, tile, tile, tile,
                  pl.BlockSpec(dmask.shape, const3), pl.BlockSpec(xi.shape, const3),
                  pl.BlockSpec(zeta.shape, const3), pl.BlockSpec(dec.shape, const3)],
        out_specs=tile,
        out_shape=jax.ShapeDtypeStruct(rq.shape, BF16),
        scratch_shapes=[pltpu.VMEM((R_HEADS, LANES, LANES), F32)],
        compiler_params=pltpu.CompilerParams(
            dimension_semantics=("arbitrary", "arbitrary"), vmem_limit_bytes=VMEM_LIMIT),
        name="retention",
    )(rq, rk, rv, rz, dmask, xi, zeta, dec)


def _merge_kernel(x_ref, a_ref, c_ref, r_ref, g_ref, wg_ref, wb_ref, wo_ref, fg_ref, o_ref,
                  *, final_norm):
    x = x_ref[...]
    d = x.shape[1]
    h = (_rms(x) * g_ref[...]).astype(BF16)
    merged = None
    for bi, br_ref in enumerate((a_ref, c_ref, r_ref)):
        t = jnp.dot(h, wg_ref[:, bi * d:(bi + 1) * d], preferred_element_type=F32)
        gate = 1.0 / (1.0 + jnp.exp(-t))
        contrib = gate * jnp.dot(br_ref[...], wb_ref[bi], preferred_element_type=F32)
        merged = contrib if merged is None else merged + contrib
    y = x + jnp.dot(merged.astype(BF16), wo_ref[...], preferred_element_type=F32)
    if final_norm:
        y = _rms(y) * fg_ref[...]
    o_ref[...] = y


def _merge(x2, a, c, r, norm_g, wg, wb, wo, final_g, final_norm):
    n, d = x2.shape
    tm = PROJ_ROWS
    row = lambda i: (i, 0)
    const = lambda i: (0, 0)
    br = pl.BlockSpec((tm, GROUP), row)
    return pl.pallas_call(
        functools.partial(_merge_kernel, final_norm=final_norm),
        grid=(n // tm,),
        in_specs=[pl.BlockSpec((tm, d), row), br, br, br,
                  pl.BlockSpec((1, d), const),
                  pl.BlockSpec(wg.shape, const),
                  pl.BlockSpec(wb.shape, lambda i: (0, 0, 0)),
                  pl.BlockSpec(wo.shape, const),
                  pl.BlockSpec((1, d), const)],
        out_specs=pl.BlockSpec((tm, d), row),
        out_shape=jax.ShapeDtypeStruct((n, d), F32),
        compiler_params=pltpu.CompilerParams(
            dimension_semantics=("arbitrary",), vmem_limit_bytes=VMEM_LIMIT),
        name="merge_outproj",
    )(x2, a, c, r, norm_g, wg, wb, wo, final_g)


def _attn_rope_table(seq):
    pos = jnp.arange(seq, dtype=F32)
    inv = ROPE_THETA ** (-jnp.arange(0, ROPE_DIM, 2, dtype=F32) / ROPE_DIM)
    ang = pos[:, None] * inv[None, :]
    half = ROPE_DIM // 2
    cos = jnp.tile(jnp.cos(ang), (1, LANES // half))
    sin = jnp.tile(jnp.sin(ang), (1, LANES // half))
    dd = (jnp.arange(LANES) % A_QK_DIM)[None, :]
    c = jnp.where(dd < ROPE_DIM, cos, 1.0)
    s_up = jnp.where(dd < half, -sin, 0.0)
    s_dn = jnp.where((dd >= half) & (dd < ROPE_DIM), sin, 0.0)
    k_tab = jnp.concatenate([c, s_up, s_dn], axis=1)
    q_scale = (A_QK_DIM ** -0.5) * math.log2(math.e)
    return jnp.concatenate([k_tab * q_scale, k_tab], axis=1).astype(F32)


def _ret_rope_table(seq):
    pos = jnp.arange(seq, dtype=F32)
    inv = 1.0 / (RET_THETA ** jnp.linspace(0.0, 1.0, R_QK_DIM // 2, dtype=F32))
    ang = pos[:, None] * inv[None, :]
    half = R_QK_DIM // 2
    c = jnp.tile(jnp.cos(ang), (1, LANES // half))
    sin = jnp.tile(jnp.sin(ang), (1, LANES // half))
    dd = (jnp.arange(LANES) % R_QK_DIM)[None, :]
    s_up = jnp.where(dd < half, -sin, 0.0)
    s_dn = jnp.where(dd >= half, sin, 0.0)
    scale = jnp.where(jnp.arange(LANES) < R_QK_DIM, 1.0, R_QK_DIM ** -0.5)[None, :]
    return jnp.concatenate([c * scale, s_up * scale, s_dn * scale], axis=1).astype(F32)


def _ret_decay_tables(chunk):
    log_g = jnp.log(1.0 - 2.0 ** (-5.0 - jnp.arange(R_HEADS, dtype=F32)))
    idx = jnp.arange(chunk, dtype=F32)
    diff = idx[:, None] - idx[None, :]
    dmask = jnp.where(diff >= 0,
                      jnp.exp(jnp.where(diff >= 0, diff, 0.0)[None] * log_g[:, None, None]),
                      0.0)
    ones = jnp.ones((1, 1, LANES), F32)
    xi = jnp.exp((idx + 1.0)[None, :] * log_g[:, None])[:, :, None] * ones
    zeta = jnp.exp((chunk - 1 - idx)[None, :] * log_g[:, None])[:, :, None] * ones
    dec = jnp.exp(chunk * log_g)[:, None, None] * ones
    return dmask.astype(F32), xi.astype(F32), zeta.astype(F32), dec.astype(F32)


def _split_w_in(w_in_l):
    d = w_in_l.shape[0]
    g = GROUP
    mix = w_in_l[:, :8 * g]
    rq = w_in_l[:, 8 * g:8 * g + g // 2].reshape(d, R_HEADS, R_QK_DIM)
    rk = w_in_l[:, 8 * g + g // 2:9 * g].reshape(d, R_HEADS, R_QK_DIM)
    rqk = jnp.concatenate([rq, rk], axis=2).reshape(d, g)
    rest = w_in_l[:, 9 * g:11 * g]
    w1 = jnp.concatenate([mix, rqk, rest], axis=1).astype(BF16)
    wg = w_in_l[:, 11 * g:].astype(BF16)
    return w1, wg


def kernel(x, norm_g, w_in, attn_lambda, attn_subln_g, conv_w, w_branch, w_out, final_norm_g):
    b, s, d = x.shape
    depth = w_in.shape[0]
    x2 = x.reshape(b * s, d)
    atab = _attn_rope_table(s)
    rtab = _ret_rope_table(s)
    dmask, xi, zeta, dec = _ret_decay_tables(RET_CHUNK)
    final_g = final_norm_g.reshape(1, d)
    for layer in range(depth):
        lam_init = 0.8 - 0.6 * math.exp(-0.3 * layer)
        w1, wg = _split_w_in(w_in[layer])
        g = norm_g[layer].reshape(1, d)
        aq, ak, av, az, c, rq, rk, rv, rz = _inproj(x2, g, w1, atab, rtab, conv_w[layer], s)
        sh = (b, s, GROUP)
        a = _attention(aq.reshape(sh), ak.reshape(sh), av.reshape(sh), az.reshape(sh),
                       attn_lambda[layer], attn_subln_g[layer].reshape(1, LANES), lam_init)
        r = _retention(rq.reshape(sh), rk.reshape(sh), rv.reshape(sh), rz.reshape(sh),
                       dmask, xi, zeta, dec)
        x2 = _merge(x2, a.reshape(b * s, GROUP), c, r.reshape(b * s, GROUP), g, wg,
                    w_branch[layer].astype(BF16), w_out[layer].astype(BF16), final_g,
                    layer == depth - 1)
    return x2.reshape(b, s, d)
```

```python
import functools
import math

import jax
import jax.numpy as jnp
from jax import lax
from jax.experimental import pallas as pl
from jax.experimental.pallas import tpu as pltpu

F32 = jnp.float32
BF16 = jnp.bfloat16

A_HEADS = 4
A_QK_DIM = 64
ROPE_THETA = 500000.0
ROPE_DIM = A_QK_DIM // 4
NEG_INF = -1e30
CONV_WIDTH = 3
R_HEADS = 4
R_QK_DIM = 64
RET_THETA = 10000.0
N_BRANCH = 3
EPS = 1e-6

LANES = 128
GROUP = 512
VMEM_LIMIT = 56 * 1024 * 1024

PROJ_ROWS = 512
ATT_TILE = 1024
RET_CHUNK = 512


def _silu(t):
    return t * (1.0 / (1.0 + jnp.exp(-t)))


def _rms(x):
    return x * lax.rsqrt(jnp.mean(x * x, axis=-1, keepdims=True) + EPS)


def _rope_block(blk, c, s_up, s_dn, shift):
    return (blk * c + pltpu.roll(blk, LANES - shift, 1) * s_up
            + pltpu.roll(blk, shift, 1) * s_dn)


def _inproj_kernel(x_ref, g_ref, w_ref, atab_ref, rtab_ref, cw_ref,
                   aq_ref, ak_ref, av_ref, az_ref, c_ref,
                   rq_ref, rk_ref, rv_ref, rz_ref, ubuf_ref, *, tiles_per_seq):
    i = pl.program_id(0)
    tm = x_ref.shape[0]
    h = (_rms(x_ref[...]) * g_ref[...]).astype(BF16)

    def proj(gi):
        return jnp.dot(h, w_ref[:, gi * GROUP:(gi + 1) * GROUP], preferred_element_type=F32)

    for gi, out_ref, t0 in ((0, aq_ref, 0), (1, ak_ref, 3 * LANES)):
        y = proj(gi)
        c = atab_ref[:, t0:t0 + LANES]
        s_up = atab_ref[:, t0 + LANES:t0 + 2 * LANES]
        s_dn = atab_ref[:, t0 + 2 * LANES:t0 + 3 * LANES]
        for hh in range(A_HEADS):
            sl = slice(hh * LANES, (hh + 1) * LANES)
            out_ref[:, sl] = _rope_block(y[:, sl], c, s_up, s_dn, ROPE_DIM // 2).astype(BF16)
    av_ref[...] = proj(2).astype(BF16)
    az_ref[...] = _silu(proj(3)).astype(BF16)

    u = proj(4) * proj(6)

    @pl.when(i % tiles_per_seq == 0)
    def _():
        ubuf_ref[0:8, :] = jnp.zeros((8, GROUP), F32)

    ubuf_ref[8:8 + tm, :] = u
    conv = (cw_ref[0:1, :] * ubuf_ref[6:6 + tm, :] + cw_ref[1:2, :] * ubuf_ref[7:7 + tm, :]
            + cw_ref[2:3, :] * u)
    ubuf_ref[0:8, :] = u[tm - 8:tm, :]
    c_ref[...] = (proj(5) * conv * _silu(proj(7))).astype(BF16)

    y = proj(8)
    c = rtab_ref[:, 0:LANES]
    s_up = rtab_ref[:, LANES:2 * LANES]
    s_dn = rtab_ref[:, 2 * LANES:3 * LANES]
    lane = lax.broadcasted_iota(jnp.int32, (tm, LANES), 1)
    low = lane < R_QK_DIM
    for hh in range(R_HEADS):
        sl = slice(hh * LANES, (hh + 1) * LANES)
        rot = _rope_block(y[:, sl], c, s_up, s_dn, R_QK_DIM // 2)
        rq_ref[:, sl] = jnp.where(low, rot, 0.0).astype(BF16)
        rk_ref[:, sl] = jnp.where(low, pltpu.roll(rot, R_QK_DIM, 1), 0.0).astype(BF16)
    rv_ref[...] = proj(9).astype(BF16)
    rz_ref[...] = _silu(proj(10)).astype(BF16)


def _inproj(x2, norm_g, w1, atab, rtab, conv_w, seq):
    n, d = x2.shape
    tm = PROJ_ROWS
    tiles_per_seq = seq // tm
    row = lambda i: (i, 0)
    pos = lambda i: (i % tiles_per_seq, 0)
    const = lambda i: (0, 0)
    out_sds = jax.ShapeDtypeStruct((n, GROUP), BF16)
    out_spec = pl.BlockSpec((tm, GROUP), row)
    return pl.pallas_call(
        functools.partial(_inproj_kernel, tiles_per_seq=tiles_per_seq),
        grid=(n // tm,),
        in_specs=[
            pl.BlockSpec((tm, d), row),
            pl.BlockSpec((1, d), const),
            pl.BlockSpec(w1.shape, const),
            pl.BlockSpec((tm, atab.shape[1]), pos),
            pl.BlockSpec((tm, rtab.shape[1]), pos),
            pl.BlockSpec(conv_w.shape, const),
        ],
        out_specs=[out_spec] * 9,
        out_shape=[out_sds] * 9,
        scratch_shapes=[pltpu.VMEM((tm + 8, GROUP), F32)],
        compiler_params=pltpu.CompilerParams(
            dimension_semantics=("arbitrary",), vmem_limit_bytes=VMEM_LIMIT),
        name="inproj",
    )(x2, norm_g, w1, atab, rtab, conv_w)


def _attn_kernel(lam_ref, g_ref, q_ref, k_ref, v_ref, z_ref, o_ref,
                 qs_ref, m_ref, l_ref, acc_ref, *, lam_init):
    i = pl.program_id(2)
    tq = q_ref.shape[0]
    q = q_ref[...]
    lane = lax.broadcasted_iota(jnp.int32, q.shape, 1)
    zero = jnp.zeros_like(q)
    qs_ref[0:tq, :] = jnp.where(lane < A_QK_DIM, q, zero)
    qs_ref[tq:2 * tq, :] = jnp.where(lane >= A_QK_DIM, q, zero)
    m_ref[...] = jnp.full(m_ref.shape, NEG_INF, F32)
    l_ref[...] = jnp.zeros(l_ref.shape, F32)
    acc_ref[...] = jnp.zeros(acc_ref.shape, F32)

    def step(j, width, masked=False):
        rows = pl.ds(pl.multiple_of(j * tq, tq), width)
        s = lax.dot_general(qs_ref[...], k_ref[rows, :], (((1,), (1,)), ((), ())),
                            preferred_element_type=F32)
        if masked:
            r = lax.broadcasted_iota(jnp.int32, s.shape, 0)
            cidx = lax.broadcasted_iota(jnp.int32, s.shape, 1)
            s = jnp.where(cidx <= jnp.where(r >= tq, r - tq, r), s, NEG_INF)
        m_prev = m_ref[...]
        m_new = jnp.maximum(m_prev, jnp.max(s, axis=1, keepdims=True))
        alpha = jnp.exp2(m_prev - m_new)
        m_ref[...] = m_new
        p = jnp.concatenate(
            [jnp.exp2(s[:, c * LANES:(c + 1) * LANES] - m_new).astype(BF16)
             for c in range(width // LANES)], axis=1)
        vs = v_ref[rows, :]
        v1 = jnp.concatenate([vs, jnp.ones_like(vs)], axis=1)
        for comp in range(2):
            rsl = slice(comp * tq, (comp + 1) * tq)
            res = jnp.dot(p[rsl], v1, preferred_element_type=F32)
            acc_ref[rsl, :] = alpha[rsl] * acc_ref[rsl, :] + res[:, :LANES]
            l_ref[rsl, :] = alpha[rsl] * l_ref[rsl, :] + res[:, LANES:]

    def wide(kk, carry):
        step(2 * kk, 2 * tq)
        return carry

    lax.fori_loop(0, i // 2, wide, 0)

    @pl.when(i % 2 == 1)
    def _():
        step(i - 1, tq)

    step(i, tq, masked=True)

    o = acc_ref[...] / l_ref[...]
    al = lam_ref[...]
    lam = (jnp.exp(jnp.sum(al[0:1] * al[1:2], axis=1, keepdims=True))
           - jnp.exp(jnp.sum(al[2:3] * al[3:4], axis=1, keepdims=True)) + lam_init)
    d = o[:tq] - lam * o[tq:]
    d = _rms(d) * g_ref[...] * (1.0 - lam_init)
    o_ref[...] = (d * z_ref[...].astype(F32)).astype(BF16)


def _attention(aq, ak, av, az, attn_lambda, subln_g, lam_init):
    b, s, _ = aq.shape
    tq = ATT_TILE
    tile = pl.BlockSpec((None, tq, LANES), lambda bi, hi, qi: (bi, qi, hi))
    full = pl.BlockSpec((None, s, LANES), lambda bi, hi, qi: (bi, 0, hi))
    const = lambda bi, hi, qi: (0, 0)
    return pl.pallas_call(
        functools.partial(_attn_kernel, lam_init=lam_init),
        grid=(b, A_HEADS, s // tq),
        in_specs=[
            pl.BlockSpec(attn_lambda.shape, const),
            pl.BlockSpec(subln_g.shape, const),
            tile, full, full, tile,
        ],
        out_specs=tile,
        out_shape=jax.ShapeDtypeStruct(aq.shape, BF16),
        scratch_shapes=[pltpu.VMEM((2 * tq, LANES), BF16),
                        pltpu.VMEM((2 * tq, LANES), F32), pltpu.VMEM((2 * tq, LANES), F32),
                        pltpu.VMEM((2 * tq, LANES), F32)],
        compiler_params=pltpu.CompilerParams(
            dimension_semantics=("arbitrary", "arbitrary", "arbitrary"),
            vmem_limit_bytes=VMEM_LIMIT),
        name="diff_attention",
    )(attn_lambda, subln_g, aq, ak, av, az)


def _ret_kernel(q_ref, k_ref, v_ref, z_ref, dm_ref, xi_ref, zeta_ref, dec_ref, o_ref, st_ref):
    @pl.when(pl.program_id(1) == 0)
    def _():
        st_ref[...] = jnp.zeros(st_ref.shape, F32)

    for hh in range(R_HEADS):
        sl = slice(hh * LANES, (hh + 1) * LANES)
        q = q_ref[:, sl]
        k = k_ref[:, sl]
        v = v_ref[:, sl]
        s = lax.dot_general(q, k, (((1,), (1,)), ((), ())), preferred_element_type=F32)
        inner = jnp.dot((s * dm_ref[hh]).astype(BF16), v, preferred_element_type=F32)
        st = st_ref[hh]
        cross = jnp.dot(q, st.astype(BF16), preferred_element_type=F32) * xi_ref[hh]
        o = _rms(inner + cross)
        o_ref[:, sl] = (o * z_ref[:, sl].astype(F32)).astype(BF16)
        kz = (k.astype(F32) * zeta_ref[hh]).astype(BF16)
        kv = lax.dot_general(kz, v, (((0,), (0,)), ((), ())), preferred_element_type=F32)
        st_ref[hh] = dec_ref[hh] * st + kv


def _retention(rq, rk, rv, rz, dmask, xi, zeta, dec):
    b, s, w = rq.shape
    c = RET_CHUNK
    tile = pl.BlockSpec((None, c, w), lambda bi, ti: (bi, ti, 0))
    const3 = lambda bi, ti: (0, 0, 0)
    return pl.pallas_call(
        _ret_kernel,
        grid=(b, s // c),
        in_specs=[tile, tile, tile, tile,
                  pl.BlockSpec(dmask.shape, const3), pl.BlockSpec(xi.shape, const3),
                  pl.BlockSpec(zeta.shape, const3), pl.BlockSpec(dec.shape, const3)],
        out_specs=tile,
        out_shape=jax.ShapeDtypeStruct(rq.shape, BF16),
        scratch_shapes=[pltpu.VMEM((R_HEADS, LANES, LANES), F32)],
        compiler_params=pltpu.CompilerParams(
            dimension_semantics=("arbitrary", "arbitrary"), vmem_limit_bytes=VMEM_LIMIT),
        name="retention",
    )(rq, rk, rv, rz, dmask, xi, zeta, dec)


def _merge_kernel(x_ref, a_ref, c_ref, r_ref, g_ref, wg_ref, wb_ref, wo_ref, fg_ref, o_ref,
                  *, final_norm):
    x = x_ref[...]
    d = x.shape[1]
    h = (_rms(x) * g_ref[...]).astype(BF16)
    merged = None
    for bi, br_ref in enumerate((a_ref, c_ref, r_ref)):
        t = jnp.dot(h, wg_ref[:, bi * d:(bi + 1) * d], preferred_element_type=F32)
        gate = 1.0 / (1.0 + jnp.exp(-t))
        contrib = gate * jnp.dot(br_ref[...], wb_ref[bi], preferred_element_type=F32)
        merged = contrib if merged is None else merged + contrib
    y = x + jnp.dot(merged.astype(BF16), wo_ref[...], preferred_element_type=F32)
    if final_norm:
        y = _rms(y) * fg_ref[...]
    o_ref[...] = y


def _merge(x2, a, c, r, norm_g, wg, wb, wo, final_g, final_norm):
    n, d = x2.shape
    tm = PROJ_ROWS
    row = lambda i: (i, 0)
    const = lambda i: (0, 0)
    br = pl.BlockSpec((tm, GROUP), row)
    return pl.pallas_call(
        functools.partial(_merge_kernel, final_norm=final_norm),
        grid=(n // tm,),
        in_specs=[pl.BlockSpec((tm, d), row), br, br, br,
                  pl.BlockSpec((1, d), const),
                  pl.BlockSpec(wg.shape, const),
                  pl.BlockSpec(wb.shape, lambda i: (0, 0, 0)),
                  pl.BlockSpec(wo.shape, const),
                  pl.BlockSpec((1, d), const)],
        out_specs=pl.BlockSpec((tm, d), row),
        out_shape=jax.ShapeDtypeStruct((n, d), F32),
        compiler_params=pltpu.CompilerParams(
            dimension_semantics=("arbitrary",), vmem_limit_bytes=VMEM_LIMIT),
        name="merge_outproj",
    )(x2, a, c, r, norm_g, wg, wb, wo, final_g)


def _attn_rope_table(seq):
    pos = jnp.arange(seq, dtype=F32)
    inv = ROPE_THETA ** (-jnp.arange(0, ROPE_DIM, 2, dtype=F32) / ROPE_DIM)
    ang = pos[:, None] * inv[None, :]
    half = ROPE_DIM // 2
    cos = jnp.tile(jnp.cos(ang), (1, LANES // half))
    sin = jnp.tile(jnp.sin(ang), (1, LANES // half))
    dd = (jnp.arange(LANES) % A_QK_DIM)[None, :]
    c = jnp.where(dd < ROPE_DIM, cos, 1.0)
    s_up = jnp.where(dd < half, -sin, 0.0)
    s_dn = jnp.where((dd >= half) & (dd < ROPE_DIM), sin, 0.0)
    k_tab = jnp.concatenate([c, s_up, s_dn], axis=1)
    q_scale = (A_QK_DIM ** -0.5) * math.log2(math.e)
    return jnp.concatenate([k_tab * q_scale, k_tab], axis=1).astype(F32)


def _ret_rope_table(seq):
    pos = jnp.arange(seq, dtype=F32)
    inv = 1.0 / (RET_THETA ** jnp.linspace(0.0, 1.0, R_QK_DIM // 2, dtype=F32))
    ang = pos[:, None] * inv[None, :]
    half = R_QK_DIM // 2
    c = jnp.tile(jnp.cos(ang), (1, LANES // half))
    sin = jnp.tile(jnp.sin(ang), (1, LANES // half))
    dd = (jnp.arange(LANES) % R_QK_DIM)[None, :]
    s_up = jnp.where(dd < half, -sin, 0.0)
    s_dn = jnp.where(dd >= half, sin, 0.0)
    scale = jnp.where(jnp.arange(LANES) < R_QK_DIM, 1.0, R_QK_DIM ** -0.5)[None, :]
    return jnp.concatenate([c * scale, s_up * scale, s_dn * scale], axis=1).astype(F32)


def _ret_decay_tables(chunk):
    log_g = jnp.log(1.0 - 2.0 ** (-5.0 - jnp.arange(R_HEADS, dtype=F32)))
    idx = jnp.arange(chunk, dtype=F32)
    diff = idx[:, None] - idx[None, :]
    dmask = jnp.where(diff >= 0,
                      jnp.exp(jnp.where(diff >= 0, diff, 0.0)[None] * log_g[:, None, None]),
                      0.0)
    ones = jnp.ones((1, 1, LANES), F32)
    xi = jnp.exp((idx + 1.0)[None, :] * log_g[:, None])[:, :, None] * ones
    zeta = jnp.exp((chunk - 1 - idx)[None, :] * log_g[:, None])[:, :, None] * ones
    dec = jnp.exp(chunk * log_g)[:, None, None] * ones
    return dmask.astype(F32), xi.astype(F32), zeta.astype(F32), dec.astype(F32)


def _split_w_in(w_in_l):
    d = w_in_l.shape[0]
    g = GROUP
    mix = w_in_l[:, :8 * g]
    rq = w_in_l[:, 8 * g:8 * g + g // 2].reshape(d, R_HEADS, R_QK_DIM)
    rk = w_in_l[:, 8 * g + g // 2:9 * g].reshape(d, R_HEADS, R_QK_DIM)
    rqk = jnp.concatenate([rq, rk], axis=2).reshape(d, g)
    rest = w_in_l[:, 9 * g:11 * g]
    w1 = jnp.concatenate([mix, rqk, rest], axis=1).astype(BF16)
    wg = w_in_l[:, 11 * g:].astype(BF16)
    return w1, wg


def kernel(x, norm_g, w_in, attn_lambda, attn_subln_g, conv_w, w_branch, w_out, final_norm_g):
    b, s, d = x.shape
    depth = w_in.shape[0]
    x2 = x.reshape(b * s, d)
    atab = _attn_rope_table(s)
    rtab = _ret_rope_table(s)
    dmask, xi, zeta, dec = _ret_decay_tables(RET_CHUNK)
    final_g = final_norm_g.reshape(1, d)
    for layer in range(depth):
        lam_init = 0.8 - 0.6 * math.exp(-0.3 * layer)
        w1, wg = _split_w_in(w_in[layer])
        g = norm_g[layer].reshape(1, d)
        aq, ak, av, az, c, rq, rk, rv, rz = _inproj(x2, g, w1, atab, rtab, conv_w[layer], s)
        sh = (b, s, GROUP)
        a = _attention(aq.reshape(sh), ak.reshape(sh), av.reshape(sh), az.reshape(sh),
                       attn_lambda[layer], attn_subln_g[layer].reshape(1, LANES), lam_init)
        r = _retention(rq.reshape(sh), rk.reshape(sh), rv.reshape(sh), rz.reshape(sh),
                       dmask, xi, zeta, dec)
        x2 = _merge(x2, a.reshape(b * s, GROUP), c, r.reshape(b * s, GROUP), g, wg,
                    w_branch[layer].astype(BF16), w_out[layer].astype(BF16), final_g,
                    layer == depth - 1)
    return x2.reshape(b, s, d)
```

```python
import functools
import math

import jax
import jax.numpy as jnp
from jax import lax
from jax.experimental import pallas as pl
from jax.experimental.pallas import tpu as pltpu

F32 = jnp.float32
BF16 = jnp.bfloat16

A_HEADS = 4
A_QK_DIM = 64
ROPE_THETA = 500000.0
ROPE_DIM = A_QK_DIM // 4
NEG_INF = -1e30
CONV_WIDTH = 3
R_HEADS = 4
R_QK_DIM = 64
RET_THETA = 10000.0
N_BRANCH = 3
EPS = 1e-6

LANES = 128
GROUP = 512
VMEM_LIMIT = 56 * 1024 * 1024

PROJ_ROWS = 512
ATT_TILE = 1024
RET_ROWS = 1024
RET_CHUNK = 256


def _silu(t):
    return t * (1.0 / (1.0 + jnp.exp(-t)))


def _rms(x):
    return x * lax.rsqrt(jnp.mean(x * x, axis=-1, keepdims=True) + EPS)


def _rope_block(blk, c, s_up, s_dn, shift):
    return (blk * c + pltpu.roll(blk, LANES - shift, 1) * s_up
            + pltpu.roll(blk, shift, 1) * s_dn)


def _inproj_kernel(x_ref, g_ref, w_ref, atab_ref, rtab_ref, cw_ref,
                   aq_ref, ak_ref, av_ref, az_ref, c_ref,
                   rq_ref, rk_ref, rv_ref, rz_ref, ubuf_ref, *, tiles_per_seq):
    i = pl.program_id(0)
    tm = x_ref.shape[0]
    h = (_rms(x_ref[...]) * g_ref[...]).astype(BF16)

    def proj(gi):
        return jnp.dot(h, w_ref[:, gi * GROUP:(gi + 1) * GROUP], preferred_element_type=F32)

    for gi, out_ref, t0 in ((0, aq_ref, 0), (1, ak_ref, 3 * LANES)):
        y = proj(gi)
        c = atab_ref[:, t0:t0 + LANES]
        s_up = atab_ref[:, t0 + LANES:t0 + 2 * LANES]
        s_dn = atab_ref[:, t0 + 2 * LANES:t0 + 3 * LANES]
        for hh in range(A_HEADS):
            sl = slice(hh * LANES, (hh + 1) * LANES)
            out_ref[:, sl] = _rope_block(y[:, sl], c, s_up, s_dn, ROPE_DIM // 2).astype(BF16)
    av_ref[...] = proj(2).astype(BF16)
    az_ref[...] = _silu(proj(3)).astype(BF16)

    u = proj(4) * proj(6)

    @pl.when(i % tiles_per_seq == 0)
    def _():
        ubuf_ref[0:8, :] = jnp.zeros((8, GROUP), F32)

    ubuf_ref[8:8 + tm, :] = u
    conv = (cw_ref[0:1, :] * ubuf_ref[6:6 + tm, :] + cw_ref[1:2, :] * ubuf_ref[7:7 + tm, :]
            + cw_ref[2:3, :] * u)
    ubuf_ref[0:8, :] = u[tm - 8:tm, :]
    c_ref[...] = (proj(5) * conv * _silu(proj(7))).astype(BF16)

    y = proj(8)
    c = rtab_ref[:, 0:LANES]
    s_up = rtab_ref[:, LANES:2 * LANES]
    s_dn = rtab_ref[:, 2 * LANES:3 * LANES]
    lane = lax.broadcasted_iota(jnp.int32, (tm, LANES), 1)
    low = lane < R_QK_DIM
    for hh in range(R_HEADS):
        sl = slice(hh * LANES, (hh + 1) * LANES)
        rot = _rope_block(y[:, sl], c, s_up, s_dn, R_QK_DIM // 2)
        rq_ref[:, sl] = jnp.where(low, rot, 0.0).astype(BF16)
        rk_ref[:, sl] = jnp.where(low, pltpu.roll(rot, R_QK_DIM, 1), 0.0).astype(BF16)
    rv_ref[...] = proj(9).astype(BF16)
    rz_ref[...] = _silu(proj(10)).astype(BF16)


def _inproj(x2, norm_g, w1, atab, rtab, conv_w, seq):
    n, d = x2.shape
    tm = PROJ_ROWS
    tiles_per_seq = seq // tm
    row = lambda i: (i, 0)
    pos = lambda i: (i % tiles_per_seq, 0)
    const = lambda i: (0, 0)
    out_sds = jax.ShapeDtypeStruct((n, GROUP), BF16)
    out_spec = pl.BlockSpec((tm, GROUP), row)
    return pl.pallas_call(
        functools.partial(_inproj_kernel, tiles_per_seq=tiles_per_seq),
        grid=(n // tm,),
        in_specs=[
            pl.BlockSpec((tm, d), row),
            pl.BlockSpec((1, d), const),
            pl.BlockSpec(w1.shape, const),
            pl.BlockSpec((tm, atab.shape[1]), pos),
            pl.BlockSpec((tm, rtab.shape[1]), pos),
            pl.BlockSpec(conv_w.shape, const),
        ],
        out_specs=[out_spec] * 9,
        out_shape=[out_sds] * 9,
        scratch_shapes=[pltpu.VMEM((tm + 8, GROUP), F32)],
        compiler_params=pltpu.CompilerParams(
            dimension_semantics=("arbitrary",), vmem_limit_bytes=VMEM_LIMIT),
        name="inproj",
    )(x2, norm_g, w1, atab, rtab, conv_w)


def _attn_kernel(lam_ref, g_ref, q_ref, k_ref, v_ref, z_ref, o_ref,
                 qs_ref, m_ref, l_ref, acc_ref, *, lam_init):
    i = pl.program_id(2)
    tq = q_ref.shape[0]
    half = tq // 2
    lane = lax.broadcasted_iota(jnp.int32, (half, LANES), 1)
    zero = jnp.zeros((half, LANES), BF16)
    for part in range(2):
        qp = q_ref[part * half:(part + 1) * half, :]
        qs_ref[(2 * part) * half:(2 * part + 1) * half, :] = jnp.where(lane < A_QK_DIM, qp, zero)
        qs_ref[(2 * part + 1) * half:(2 * part + 2) * half, :] = jnp.where(
            lane >= A_QK_DIM, qp, zero)
    m_ref[...] = jnp.full(m_ref.shape, NEG_INF, F32)
    l_ref[...] = jnp.zeros(l_ref.shape, F32)
    acc_ref[...] = jnp.zeros(acc_ref.shape, F32)

    def step(j, width, r0=0, nrows=2 * tq, diag_shift=None):
        keys = pl.ds(pl.multiple_of(j * tq, tq), width)
        rsl = slice(r0, r0 + nrows)
        s = lax.dot_general(qs_ref[rsl, :], k_ref[keys, :], (((1,), (1,)), ((), ())),
                            preferred_element_type=F32)
        if diag_shift is not None:
            r = lax.broadcasted_iota(jnp.int32, s.shape, 0)
            cidx = lax.broadcasted_iota(jnp.int32, s.shape, 1)
            s = jnp.where(cidx <= diag_shift + jnp.where(r >= half, r - half, r), s, NEG_INF)
        m_prev = m_ref[rsl, :]
        m_new = jnp.maximum(m_prev, jnp.max(s, axis=1, keepdims=True))
        alpha = jnp.exp2(m_prev - m_new)
        m_ref[rsl, :] = m_new
        p = jnp.concatenate(
            [jnp.exp2(s[:, c * LANES:(c + 1) * LANES] - m_new).astype(BF16)
             for c in range(width // LANES)], axis=1)
        vs = v_ref[keys, :]
        v1 = jnp.concatenate([vs, jnp.ones_like(vs)], axis=1)
        for part in range(2):
            psl = slice(part * nrows // 2, (part + 1) * nrows // 2)
            osl = slice(r0 + part * nrows // 2, r0 + (part + 1) * nrows // 2)
            res = jnp.dot(p[psl], v1, preferred_element_type=F32)
            acc_ref[osl, :] = alpha[psl] * acc_ref[osl, :] + res[:, :LANES]
            l_ref[osl, :] = alpha[psl] * l_ref[osl, :] + res[:, LANES:]

    def wide(kk, carry):
        step(2 * kk, 2 * tq)
        return carry

    lax.fori_loop(0, i // 2, wide, 0)

    @pl.when(i % 2 == 1)
    def _():
        step(i - 1, tq)

    step(i, half, r0=0, nrows=tq, diag_shift=0)
    step(i, tq, r0=tq, nrows=tq, diag_shift=half)

    o = acc_ref[...] / l_ref[...]
    al = lam_ref[...]
    lam = (jnp.exp(jnp.sum(al[0:1] * al[1:2], axis=1, keepdims=True))
           - jnp.exp(jnp.sum(al[2:3] * al[3:4], axis=1, keepdims=True)) + lam_init)
    d = jnp.concatenate([o[0:half] - lam * o[half:tq],
                         o[tq:tq + half] - lam * o[tq + half:2 * tq]], axis=0)
    d = _rms(d) * g_ref[...] * (1.0 - lam_init)
    o_ref[...] = (d * z_ref[...].astype(F32)).astype(BF16)


def _attention(aq, ak, av, az, attn_lambda, subln_g, lam_init):
    b, s, _ = aq.shape
    tq = ATT_TILE
    tile = pl.BlockSpec((None, tq, LANES), lambda bi, hi, qi: (bi, qi, hi))
    full = pl.BlockSpec((None, s, LANES), lambda bi, hi, qi: (bi, 0, hi))
    const = lambda bi, hi, qi: (0, 0)
    return pl.pallas_call(
        functools.partial(_attn_kernel, lam_init=lam_init),
        grid=(b, A_HEADS, s // tq),
        in_specs=[
            pl.BlockSpec(attn_lambda.shape, const),
            pl.BlockSpec(subln_g.shape, const),
            tile, full, full, tile,
        ],
        out_specs=tile,
        out_shape=jax.ShapeDtypeStruct(aq.shape, BF16),
        scratch_shapes=[pltpu.VMEM((2 * tq, LANES), BF16),
                        pltpu.VMEM((2 * tq, LANES), F32), pltpu.VMEM((2 * tq, LANES), F32),
                        pltpu.VMEM((2 * tq, LANES), F32)],
        compiler_params=pltpu.CompilerParams(
            dimension_semantics=("arbitrary", "arbitrary", "arbitrary"),
            vmem_limit_bytes=VMEM_LIMIT),
        name="diff_attention",
    )(attn_lambda, subln_g, aq, ak, av, az)


def _ret_kernel(q_ref, k_ref, v_ref, z_ref, dm_ref, xi_ref, zeta_ref, dec_ref, o_ref, st_ref):
    @pl.when(pl.program_id(1) == 0)
    def _():
        st_ref[...] = jnp.zeros(st_ref.shape, F32)

    chunk = dm_ref.shape[1]
    for hh in range(R_HEADS):
        sl = slice(hh * LANES, (hh + 1) * LANES)
        for cc in range(q_ref.shape[0] // chunk):
            rows = slice(cc * chunk, (cc + 1) * chunk)
            q = q_ref[rows, sl]
            k = k_ref[rows, sl]
            v = v_ref[rows, sl]
            s = lax.dot_general(q, k, (((1,), (1,)), ((), ())), preferred_element_type=F32)
            inner = jnp.dot((s * dm_ref[hh]).astype(BF16), v, preferred_element_type=F32)
            st = st_ref[hh]
            cross = jnp.dot(q, st.astype(BF16), preferred_element_type=F32) * xi_ref[hh]
            o = _rms(inner + cross)
            o_ref[rows, sl] = (o * z_ref[rows, sl].astype(F32)).astype(BF16)
            kz = (k.astype(F32) * zeta_ref[hh]).astype(BF16)
            kv = lax.dot_general(kz, v, (((0,), (0,)), ((), ())), preferred_element_type=F32)
            st_ref[hh] = dec_ref[hh] * st + kv


def _retention(rq, rk, rv, rz, dmask, xi, zeta, dec):
    b, s, w = rq.shape
    c = RET_ROWS
    tile = pl.BlockSpec((None, c, w), lambda bi, ti: (bi, ti, 0))
    const3 = lambda bi, ti: (0, 0, 0)
    return pl.pallas_call(
        _ret_kernel,
        grid=(b, s // c),
        in_specs=[tile, tile, tile, tile,
                  pl.BlockSpec(dmask.shape, const3), pl.BlockSpec(xi.shape, const3),
                  pl.BlockSpec(zeta.shape, const3), pl.BlockSpec(dec.shape, const3)],
        out_specs=tile,
        out_shape=jax.ShapeDtypeStruct(rq.shape, BF16),
        scratch_shapes=[pltpu.VMEM((R_HEADS, LANES, LANES), F32)],
        compiler_params=pltpu.CompilerParams(
            dimension_semantics=("arbitrary", "arbitrary"), vmem_limit_bytes=VMEM_LIMIT),
        name="retention",
    )(rq, rk, rv, rz, dmask, xi, zeta, dec)


def _merge_kernel(x_ref, a_ref, c_ref, r_ref, g_ref, wg_ref, wb_ref, wo_ref, fg_ref, o_ref,
                  *, final_norm):
    x = x_ref[...]
    d = x.shape[1]
    h = (_rms(x) * g_ref[...]).astype(BF16)
    merged = None
    for bi, br_ref in enumerate((a_ref, c_ref, r_ref)):
        t = jnp.dot(h, wg_ref[:, bi * d:(bi + 1) * d], preferred_element_type=F32)
        gate = 1.0 / (1.0 + jnp.exp(-t))
        contrib = gate * jnp.dot(br_ref[...], wb_ref[bi], preferred_element_type=F32)
        merged = contrib if merged is None else merged + contrib
    y = x + jnp.dot(merged.astype(BF16), wo_ref[...], preferred_element_type=F32)
    if final_norm:
        y = _rms(y) * fg_ref[...]
    o_ref[...] = y


def _merge(x2, a, c, r, norm_g, wg, wb, wo, final_g, final_norm):
    n, d = x2.shape
    tm = PROJ_ROWS
    row = lambda i: (i, 0)
    const = lambda i: (0, 0)
    br = pl.BlockSpec((tm, GROUP), row)
    return pl.pallas_call(
        functools.partial(_merge_kernel, final_norm=final_norm),
        grid=(n // tm,),
        in_specs=[pl.BlockSpec((tm, d), row), br, br, br,
                  pl.BlockSpec((1, d), const),
                  pl.BlockSpec(wg.shape, const),
                  pl.BlockSpec(wb.shape, lambda i: (0, 0, 0)),
                  pl.BlockSpec(wo.shape, const),
                  pl.BlockSpec((1, d), const)],
        out_specs=pl.BlockSpec((tm, d), row),
        out_shape=jax.ShapeDtypeStruct((n, d), F32),
        compiler_params=pltpu.CompilerParams(
            dimension_semantics=("arbitrary",), vmem_limit_bytes=VMEM_LIMIT),
        name="merge_outproj",
    )(x2, a, c, r, norm_g, wg, wb, wo, final_g)


def _attn_rope_table(seq):
    pos = jnp.arange(seq, dtype=F32)
    inv = ROPE_THETA ** (-jnp.arange(0, ROPE_DIM, 2, dtype=F32) / ROPE_DIM)
    ang = pos[:, None] * inv[None, :]
    half = ROPE_DIM // 2
    cos = jnp.tile(jnp.cos(ang), (1, LANES // half))
    sin = jnp.tile(jnp.sin(ang), (1, LANES // half))
    dd = (jnp.arange(LANES) % A_QK_DIM)[None, :]
    c = jnp.where(dd < ROPE_DIM, cos, 1.0)
    s_up = jnp.where(dd < half, -sin, 0.0)
    s_dn = jnp.where((dd >= half) & (dd < ROPE_DIM), sin, 0.0)
    k_tab = jnp.concatenate([c, s_up, s_dn], axis=1)
    q_scale = (A_QK_DIM ** -0.5) * math.log2(math.e)
    return jnp.concatenate([k_tab * q_scale, k_tab], axis=1).astype(F32)


def _ret_rope_table(seq):
    pos = jnp.arange(seq, dtype=F32)
    inv = 1.0 / (RET_THETA ** jnp.linspace(0.0, 1.0, R_QK_DIM // 2, dtype=F32))
    ang = pos[:, None] * inv[None, :]
    half = R_QK_DIM // 2
    c = jnp.tile(jnp.cos(ang), (1, LANES // half))
    sin = jnp.tile(jnp.sin(ang), (1, LANES // half))
    dd = (jnp.arange(LANES) % R_QK_DIM)[None, :]
    s_up = jnp.where(dd < half, -sin, 0.0)
    s_dn = jnp.where(dd >= half, sin, 0.0)
    scale = jnp.where(jnp.arange(LANES) < R_QK_DIM, 1.0, R_QK_DIM ** -0.5)[None, :]
    return jnp.concatenate([c * scale, s_up * scale, s_dn * scale], axis=1).astype(F32)


def _ret_decay_tables(chunk):
    log_g = jnp.log(1.0 - 2.0 ** (-5.0 - jnp.arange(R_HEADS, dtype=F32)))
    idx = jnp.arange(chunk, dtype=F32)
    diff = idx[:, None] - idx[None, :]
    dmask = jnp.where(diff >= 0,
                      jnp.exp(jnp.where(diff >= 0, diff, 0.0)[None] * log_g[:, None, None]),
                      0.0)
    ones = jnp.ones((1, 1, LANES), F32)
    xi = jnp.exp((idx + 1.0)[None, :] * log_g[:, None])[:, :, None] * ones
    zeta = jnp.exp((chunk - 1 - idx)[None, :] * log_g[:, None])[:, :, None] * ones
    dec = jnp.exp(chunk * log_g)[:, None, None] * ones
    return dmask.astype(F32), xi.astype(F32), zeta.astype(F32), dec.astype(F32)


def _split_w_in(w_in_l):
    d = w_in_l.shape[0]
    g = GROUP
    mix = w_in_l[:, :8 * g]
    rq = w_in_l[:, 8 * g:8 * g + g // 2].reshape(d, R_HEADS, R_QK_DIM)
    rk = w_in_l[:, 8 * g + g // 2:9 * g].reshape(d, R_HEADS, R_QK_DIM)
    rqk = jnp.concatenate([rq, rk], axis=2).reshape(d, g)
    rest = w_in_l[:, 9 * g:11 * g]
    w1 = jnp.concatenate([mix, rqk, rest], axis=1).astype(BF16)
    wg = w_in_l[:, 11 * g:].astype(BF16)
    return w1, wg


def kernel(x, norm_g, w_in, attn_lambda, attn_subln_g, conv_w, w_branch, w_out, final_norm_g):
    b, s, d = x.shape
    depth = w_in.shape[0]
    x2 = x.reshape(b * s, d)
    atab = _attn_rope_table(s)
    rtab = _ret_rope_table(s)
    dmask, xi, zeta, dec = _ret_decay_tables(RET_CHUNK)
    final_g = final_norm_g.reshape(1, d)
    for layer in range(depth):
        lam_init = 0.8 - 0.6 * math.exp(-0.3 * layer)
        w1, wg = _split_w_in(w_in[layer])
        g = norm_g[layer].reshape(1, d)
        aq, ak, av, az, c, rq, rk, rv, rz = _inproj(x2, g, w1, atab, rtab, conv_w[layer], s)
        sh = (b, s, GROUP)
        a = _attention(aq.reshape(sh), ak.reshape(sh), av.reshape(sh), az.reshape(sh),
                       attn_lambda[layer], attn_subln_g[layer].reshape(1, LANES), lam_init)
        r = _retention(rq.reshape(sh), rk.reshape(sh), rv.reshape(sh), rz.reshape(sh),
                       dmask, xi, zeta, dec)
        x2 = _merge(x2, a.reshape(b * s, GROUP), c, r.reshape(b * s, GROUP), g, wg,
                    w_branch[layer].astype(BF16), w_out[layer].astype(BF16), final_g,
                    layer == depth - 1)
    return x2.reshape(b, s, d)
```

```python
import functools
import math

import jax
import jax.numpy as jnp
from jax import lax
from jax.experimental import pallas as pl
from jax.experimental.pallas import tpu as pltpu

F32 = jnp.float32
BF16 = jnp.bfloat16

A_HEADS = 4
A_QK_DIM = 64
ROPE_THETA = 500000.0
ROPE_DIM = A_QK_DIM // 4
NEG_INF = -1e30
CONV_WIDTH = 3
R_HEADS = 4
R_QK_DIM = 64
RET_THETA = 10000.0
N_BRANCH = 3
EPS = 1e-6

LANES = 128
GROUP = 512
VMEM_LIMIT = 56 * 1024 * 1024

PROJ_ROWS = 512
ATT_TILE = 1024
RET_ROWS = 1024
RET_CHUNK = 256


def _silu(t):
    return t * (1.0 / (1.0 + jnp.exp(-t)))


def _rms(x):
    return x * lax.rsqrt(jnp.mean(x * x, axis=-1, keepdims=True) + EPS)


def _rope_block(blk, c, s_up, s_dn, shift):
    return (blk * c + pltpu.roll(blk, LANES - shift, 1) * s_up
            + pltpu.roll(blk, shift, 1) * s_dn)


def _inproj_kernel(x_ref, g_ref, w_ref, atab_ref, rtab_ref, cw_ref,
                   aq_ref, ak_ref, av_ref, az_ref, c_ref,
                   rq_ref, rk_ref, rv_ref, rz_ref, ubuf_ref, *, tiles_per_seq):
    i = pl.program_id(0)
    tm = x_ref.shape[0]
    h = (_rms(x_ref[...]) * g_ref[...]).astype(BF16)

    def proj(gi):
        return jnp.dot(h, w_ref[:, gi * GROUP:(gi + 1) * GROUP], preferred_element_type=F32)

    c = atab_ref[:, 0:LANES]
    s_up = atab_ref[:, LANES:2 * LANES]
    s_dn = atab_ref[:, 2 * LANES:3 * LANES]
    for gi, out_ref, scale in ((0, aq_ref, (A_QK_DIM ** -0.5) * math.log2(math.e)),
                               (1, ak_ref, None)):
        y = proj(gi)
        for hh in range(A_HEADS):
            sl = slice(hh * LANES, (hh + 1) * LANES)
            rot = _rope_block(y[:, sl], c, s_up, s_dn, ROPE_DIM // 2)
            out_ref[:, sl] = (rot if scale is None else rot * scale).astype(BF16)
    av_ref[...] = proj(2).astype(BF16)
    az_ref[...] = _silu(proj(3)).astype(BF16)

    u = proj(4) * proj(6)

    @pl.when(i % tiles_per_seq == 0)
    def _():
        ubuf_ref[0:8, :] = jnp.zeros((8, GROUP), F32)

    ubuf_ref[8:8 + tm, :] = u
    conv = (cw_ref[0:1, :] * ubuf_ref[6:6 + tm, :] + cw_ref[1:2, :] * ubuf_ref[7:7 + tm, :]
            + cw_ref[2:3, :] * u)
    ubuf_ref[0:8, :] = u[tm - 8:tm, :]
    c_ref[...] = (proj(5) * conv * _silu(proj(7))).astype(BF16)

    y = proj(8)
    c = rtab_ref[:, 0:LANES]
    s_up = rtab_ref[:, LANES:2 * LANES]
    s_dn = rtab_ref[:, 2 * LANES:3 * LANES]
    lane = lax.broadcasted_iota(jnp.int32, (tm, LANES), 1)
    low = lane < R_QK_DIM
    for hh in range(R_HEADS):
        sl = slice(hh * LANES, (hh + 1) * LANES)
        rot = _rope_block(y[:, sl], c, s_up, s_dn, R_QK_DIM // 2)
        rq_ref[:, sl] = jnp.where(low, rot, 0.0).astype(BF16)
        rk_ref[:, sl] = jnp.where(low, pltpu.roll(rot, R_QK_DIM, 1), 0.0).astype(BF16)
    rv_ref[...] = proj(9).astype(BF16)
    rz_ref[...] = _silu(proj(10)).astype(BF16)


def _inproj(x2, norm_g, w1, atab, rtab, conv_w, seq):
    n, d = x2.shape
    tm = PROJ_ROWS
    tiles_per_seq = seq // tm
    row = lambda i: (i, 0)
    pos = lambda i: (i % tiles_per_seq, 0)
    const = lambda i: (0, 0)
    out_sds = jax.ShapeDtypeStruct((n, GROUP), BF16)
    out_spec = pl.BlockSpec((tm, GROUP), row)
    return pl.pallas_call(
        functools.partial(_inproj_kernel, tiles_per_seq=tiles_per_seq),
        grid=(n // tm,),
        in_specs=[
            pl.BlockSpec((tm, d), row),
            pl.BlockSpec((1, d), const),
            pl.BlockSpec(w1.shape, const),
            pl.BlockSpec((tm, atab.shape[1]), pos),
            pl.BlockSpec((tm, rtab.shape[1]), pos),
            pl.BlockSpec(conv_w.shape, const),
        ],
        out_specs=[out_spec] * 9,
        out_shape=[out_sds] * 9,
        scratch_shapes=[pltpu.VMEM((tm + 8, GROUP), F32)],
        compiler_params=pltpu.CompilerParams(
            dimension_semantics=("arbitrary",), vmem_limit_bytes=VMEM_LIMIT),
        name="inproj",
    )(x2, norm_g, w1, atab, rtab, conv_w)


def _attn_kernel(lam_ref, g_ref, q_ref, k_ref, v_ref, z_ref, o_ref,
                 qs_ref, m_ref, l_ref, acc_ref, *, lam_init):
    i = pl.program_id(2)
    tq = q_ref.shape[0]
    half = tq // 2
    lane = lax.broadcasted_iota(jnp.int32, (half, LANES), 1)
    zero = jnp.zeros((half, LANES), BF16)
    for part in range(2):
        qp = q_ref[part * half:(part + 1) * half, :]
        qs_ref[(2 * part) * half:(2 * part + 1) * half, :] = jnp.where(lane < A_QK_DIM, qp, zero)
        qs_ref[(2 * part + 1) * half:(2 * part + 2) * half, :] = jnp.where(
            lane >= A_QK_DIM, qp, zero)
    m_ref[...] = jnp.full(m_ref.shape, NEG_INF, F32)
    l_ref[...] = jnp.zeros(l_ref.shape, F32)
    acc_ref[...] = jnp.zeros(acc_ref.shape, F32)

    def scores(j, width, r0=0, nrows=2 * tq, diag_shift=None):
        keys = pl.ds(pl.multiple_of(j * tq, tq), width)
        s = lax.dot_general(qs_ref[r0:r0 + nrows, :], k_ref[keys, :], (((1,), (1,)), ((), ())),
                            preferred_element_type=F32)
        if diag_shift is not None:
            r = lax.broadcasted_iota(jnp.int32, s.shape, 0)
            cidx = lax.broadcasted_iota(jnp.int32, s.shape, 1)
            s = jnp.where(cidx <= diag_shift + jnp.where(r >= half, r - half, r), s, NEG_INF)
        return s

    def update(s, j, r0=0):
        nrows, width = s.shape
        keys = pl.ds(pl.multiple_of(j * tq, tq), width)
        rsl = slice(r0, r0 + nrows)
        m_prev = m_ref[rsl, :]
        m_new = jnp.maximum(m_prev, jnp.max(s, axis=1, keepdims=True))
        alpha = jnp.exp2(m_prev - m_new)
        m_ref[rsl, :] = m_new
        p = jnp.concatenate(
            [jnp.exp2(s[:, c * LANES:(c + 1) * LANES] - m_new).astype(BF16)
             for c in range(width // LANES)], axis=1)
        vs = v_ref[keys, :]
        v1 = jnp.concatenate([vs, jnp.ones_like(vs)], axis=1)
        for part in range(2):
            psl = slice(part * nrows // 2, (part + 1) * nrows // 2)
            osl = slice(r0 + part * nrows // 2, r0 + (part + 1) * nrows // 2)
            res = jnp.dot(p[psl], v1, preferred_element_type=F32)
            acc_ref[osl, :] = alpha[psl] * acc_ref[osl, :] + res[:, :LANES]
            l_ref[osl, :] = alpha[psl] * l_ref[osl, :] + res[:, LANES:]

    def wide(kk, carry):
        update(scores(2 * kk, 2 * tq), 2 * kk)
        return carry

    lax.fori_loop(0, i // 2, wide, 0)

    @pl.when(i % 2 == 1)
    def _():
        update(scores(i - 1, tq), i - 1)

    s_lo = scores(i, half, r0=0, nrows=tq, diag_shift=0)
    s_hi = scores(i, tq, r0=tq, nrows=tq, diag_shift=half)
    update(s_lo, i, r0=0)
    update(s_hi, i, r0=tq)

    o = acc_ref[...] / l_ref[...]
    al = lam_ref[...]
    lam = (jnp.exp(jnp.sum(al[0:1] * al[1:2], axis=1, keepdims=True))
           - jnp.exp(jnp.sum(al[2:3] * al[3:4], axis=1, keepdims=True)) + lam_init)
    d = jnp.concatenate([o[0:half] - lam * o[half:tq],
                         o[tq:tq + half] - lam * o[tq + half:2 * tq]], axis=0)
    d = _rms(d) * g_ref[...] * (1.0 - lam_init)
    o_ref[...] = (d * z_ref[...].astype(F32)).astype(BF16)


def _attention(aq, ak, av, az, attn_lambda, subln_g, lam_init):
    b, s, _ = aq.shape
    tq = ATT_TILE
    tile = pl.BlockSpec((None, tq, LANES), lambda bi, hi, qi: (bi, qi, hi))
    full = pl.BlockSpec((None, s, LANES), lambda bi, hi, qi: (bi, 0, hi))
    const = lambda bi, hi, qi: (0, 0)
    return pl.pallas_call(
        functools.partial(_attn_kernel, lam_init=lam_init),
        grid=(b, A_HEADS, s // tq),
        in_specs=[
            pl.BlockSpec(attn_lambda.shape, const),
            pl.BlockSpec(subln_g.shape, const),
            tile, full, full, tile,
        ],
        out_specs=tile,
        out_shape=jax.ShapeDtypeStruct(aq.shape, BF16),
        scratch_shapes=[pltpu.VMEM((2 * tq, LANES), BF16),
                        pltpu.VMEM((2 * tq, LANES), F32), pltpu.VMEM((2 * tq, LANES), F32),
                        pltpu.VMEM((2 * tq, LANES), F32)],
        compiler_params=pltpu.CompilerParams(
            dimension_semantics=("arbitrary", "arbitrary", "arbitrary"),
            vmem_limit_bytes=VMEM_LIMIT),
        name="diff_attention",
    )(attn_lambda, subln_g, aq, ak, av, az)


def _ret_kernel(q_ref, k_ref, v_ref, z_ref, dm_ref, xi_ref, zeta_ref, dec_ref, o_ref, st_ref):
    @pl.when(pl.program_id(1) == 0)
    def _():
        st_ref[...] = jnp.zeros(st_ref.shape, F32)

    chunk = dm_ref.shape[1]
    for hh in range(R_HEADS):
        sl = slice(hh * LANES, (hh + 1) * LANES)
        for cc in range(q_ref.shape[0] // chunk):
            rows = slice(cc * chunk, (cc + 1) * chunk)
            q = q_ref[rows, sl]
            k = k_ref[rows, sl]
            v = v_ref[rows, sl]
            s = lax.dot_general(q, k, (((1,), (1,)), ((), ())), preferred_element_type=F32)
            inner = jnp.dot((s * dm_ref[hh]).astype(BF16), v, preferred_element_type=F32)
            st = st_ref[hh]
            cross = jnp.dot(q, st.astype(BF16), preferred_element_type=F32) * xi_ref[hh]
            o = _rms(inner + cross)
            o_ref[rows, sl] = (o * z_ref[rows, sl].astype(F32)).astype(BF16)
            kz = (k.astype(F32) * zeta_ref[hh]).astype(BF16)
            kv = lax.dot_general(kz, v, (((0,), (0,)), ((), ())), preferred_element_type=F32)
            st_ref[hh] = dec_ref[hh] * st + kv


def _retention(rq, rk, rv, rz, dmask, xi, zeta, dec):
    b, s, w = rq.shape
    c = RET_ROWS
    tile = pl.BlockSpec((None, c, w), lambda bi, ti: (bi, ti, 0))
    const3 = lambda bi, ti: (0, 0, 0)
    return pl.pallas_call(
        _ret_kernel,
        grid=(b, s // c),
        in_specs=[tile, tile, tile, tile,
                  pl.BlockSpec(dmask.shape, const3), pl.BlockSpec(xi.shape, const3),
                  pl.BlockSpec(zeta.shape, const3), pl.BlockSpec(dec.shape, const3)],
        out_specs=tile,
        out_shape=jax.ShapeDtypeStruct(rq.shape, BF16),
        scratch_shapes=[pltpu.VMEM((R_HEADS, LANES, LANES), F32)],
        compiler_params=pltpu.CompilerParams(
            dimension_semantics=("arbitrary", "arbitrary"), vmem_limit_bytes=VMEM_LIMIT),
        name="retention",
    )(rq, rk, rv, rz, dmask, xi, zeta, dec)


def _merge_kernel(x_ref, a_ref, c_ref, r_ref, g_ref, wg_ref, wb_ref, wo_ref, fg_ref, o_ref,
                  *, final_norm):
    x = x_ref[...]
    d = x.shape[1]
    h = (_rms(x) * g_ref[...]).astype(BF16)
    merged = None
    for bi, br_ref in enumerate((a_ref, c_ref, r_ref)):
        t = jnp.dot(h, wg_ref[:, bi * d:(bi + 1) * d], preferred_element_type=F32)
        gate = 1.0 / (1.0 + jnp.exp(-t))
        contrib = gate * jnp.dot(br_ref[...], wb_ref[bi], preferred_element_type=F32)
        merged = contrib if merged is None else merged + contrib
    y = x + jnp.dot(merged.astype(BF16), wo_ref[...], preferred_element_type=F32)
    if final_norm:
        y = _rms(y) * fg_ref[...]
    o_ref[...] = y


def _merge(x2, a, c, r, norm_g, wg, wb, wo, final_g, final_norm):
    n, d = x2.shape
    tm = PROJ_ROWS
    row = lambda i: (i, 0)
    const = lambda i: (0, 0)
    br = pl.BlockSpec((tm, GROUP), row)
    return pl.pallas_call(
        functools.partial(_merge_kernel, final_norm=final_norm),
        grid=(n // tm,),
        in_specs=[pl.BlockSpec((tm, d), row), br, br, br,
                  pl.BlockSpec((1, d), const),
                  pl.BlockSpec(wg.shape, const),
                  pl.BlockSpec(wb.shape, lambda i: (0, 0, 0)),
                  pl.BlockSpec(wo.shape, const),
                  pl.BlockSpec((1, d), const)],
        out_specs=pl.BlockSpec((tm, d), row),
        out_shape=jax.ShapeDtypeStruct((n, d), F32),
        compiler_params=pltpu.CompilerParams(
            dimension_semantics=("arbitrary",), vmem_limit_bytes=VMEM_LIMIT),
        name="merge_outproj",
    )(x2, a, c, r, norm_g, wg, wb, wo, final_g)


def _attn_rope_table(seq):
    pos = jnp.arange(seq, dtype=F32)
    inv = ROPE_THETA ** (-jnp.arange(0, ROPE_DIM, 2, dtype=F32) / ROPE_DIM)
    ang = pos[:, None] * inv[None, :]
    half = ROPE_DIM // 2
    cos = jnp.tile(jnp.cos(ang), (1, LANES // half))
    sin = jnp.tile(jnp.sin(ang), (1, LANES // half))
    dd = (jnp.arange(LANES) % A_QK_DIM)[None, :]
    c = jnp.where(dd < ROPE_DIM, cos, 1.0)
    s_up = jnp.where(dd < half, -sin, 0.0)
    s_dn = jnp.where((dd >= half) & (dd < ROPE_DIM), sin, 0.0)
    return jnp.concatenate([c, s_up, s_dn], axis=1).astype(F32)


def _ret_rope_table(seq):
    pos = jnp.arange(seq, dtype=F32)
    inv = 1.0 / (RET_THETA ** jnp.linspace(0.0, 1.0, R_QK_DIM // 2, dtype=F32))
    ang = pos[:, None] * inv[None, :]
    half = R_QK_DIM // 2
    c = jnp.tile(jnp.cos(ang), (1, LANES // half))
    sin = jnp.tile(jnp.sin(ang), (1, LANES // half))
    dd = (jnp.arange(LANES) % R_QK_DIM)[None, :]
    s_up = jnp.where(dd < half, -sin, 0.0)
    s_dn = jnp.where(dd >= half, sin, 0.0)
    scale = jnp.where(jnp.arange(LANES) < R_QK_DIM, 1.0, R_QK_DIM ** -0.5)[None, :]
    return jnp.concatenate([c * scale, s_up * scale, s_dn * scale], axis=1).astype(F32)


def _ret_decay_tables(chunk):
    log_g = jnp.log(1.0 - 2.0 ** (-5.0 - jnp.arange(R_HEADS, dtype=F32)))
    idx = jnp.arange(chunk, dtype=F32)
    diff = idx[:, None] - idx[None, :]
    dmask = jnp.where(diff >= 0,
                      jnp.exp(jnp.where(diff >= 0, diff, 0.0)[None] * log_g[:, None, None]),
                      0.0)
    ones = jnp.ones((1, 1, LANES), F32)
    xi = jnp.exp((idx + 1.0)[None, :] * log_g[:, None])[:, :, None] * ones
    zeta = jnp.exp((chunk - 1 - idx)[None, :] * log_g[:, None])[:, :, None] * ones
    dec = jnp.exp(chunk * log_g)[:, None, None] * ones
    return dmask.astype(F32), xi.astype(F32), zeta.astype(F32), dec.astype(F32)


def _split_w_in(w_in_l):
    d = w_in_l.shape[0]
    g = GROUP
    mix = w_in_l[:, :8 * g]
    rq = w_in_l[:, 8 * g:8 * g + g // 2].reshape(d, R_HEADS, R_QK_DIM)
    rk = w_in_l[:, 8 * g + g // 2:9 * g].reshape(d, R_HEADS, R_QK_DIM)
    rqk = jnp.concatenate([rq, rk], axis=2).reshape(d, g)
    rest = w_in_l[:, 9 * g:11 * g]
    w1 = jnp.concatenate([mix, rqk, rest], axis=1).astype(BF16)
    wg = w_in_l[:, 11 * g:].astype(BF16)
    return w1, wg


def kernel(x, norm_g, w_in, attn_lambda, attn_subln_g, conv_w, w_branch, w_out, final_norm_g):
    b, s, d = x.shape
    depth = w_in.shape[0]
    x2 = x.reshape(b * s, d)
    atab = _attn_rope_table(s)
    rtab = _ret_rope_table(s)
    dmask, xi, zeta, dec = _ret_decay_tables(RET_CHUNK)
    final_g = final_norm_g.reshape(1, d)
    for layer in range(depth):
        lam_init = 0.8 - 0.6 * math.exp(-0.3 * layer)
        w1, wg = _split_w_in(w_in[layer])
        g = norm_g[layer].reshape(1, d)
        aq, ak, av, az, c, rq, rk, rv, rz = _inproj(x2, g, w1, atab, rtab, conv_w[layer], s)
        sh = (b, s, GROUP)
        a = _attention(aq.reshape(sh), ak.reshape(sh), av.reshape(sh), az.reshape(sh),
                       attn_lambda[layer], attn_subln_g[layer].reshape(1, LANES), lam_init)
        r = _retention(rq.reshape(sh), rk.reshape(sh), rv.reshape(sh), rz.reshape(sh),
                       dmask, xi, zeta, dec)
        x2 = _merge(x2, a.reshape(b * s, GROUP), c, r.reshape(b * s, GROUP), g, wg,
                    w_branch[layer].astype(BF16), w_out[layer].astype(BF16), final_g,
                    layer == depth - 1)
    return x2.reshape(b, s, d)
```

```python
import functools
import math

import jax
import jax.numpy as jnp
from jax import lax
from jax.experimental import pallas as pl
from jax.experimental.pallas import tpu as pltpu

F32 = jnp.float32
BF16 = jnp.bfloat16

A_HEADS = 4
A_QK_DIM = 64
ROPE_THETA = 500000.0
ROPE_DIM = A_QK_DIM // 4
NEG_INF = -1e30
CONV_WIDTH = 3
R_HEADS = 4
R_QK_DIM = 64
RET_THETA = 10000.0
N_BRANCH = 3
EPS = 1e-6

LANES = 128
GROUP = 512
VMEM_LIMIT = 56 * 1024 * 1024

PROJ_ROWS = 512
MERGE_ROWS = 512
ATT_TILE = 1024
RET_ROWS = 1024
RET_CHUNK = 256


def _silu(t):
    return t * (1.0 / (1.0 + jnp.exp(-t)))


def _rms(x):
    return x * lax.rsqrt(jnp.mean(x * x, axis=-1, keepdims=True) + EPS)


def _rope_block(blk, c, s_up, s_dn, shift):
    return (blk * c + pltpu.roll(blk, LANES - shift, 1) * s_up
            + pltpu.roll(blk, shift, 1) * s_dn)


def _inproj_kernel(x_ref, g_ref, w_ref, atab_ref, rtab_ref, cw_ref,
                   aq_ref, ak_ref, av_ref, az_ref, c_ref,
                   rq_ref, rk_ref, rv_ref, rz_ref, ubuf_ref, *, tiles_per_seq):
    i = pl.program_id(0)
    tm = x_ref.shape[0]
    h = (_rms(x_ref[...]) * g_ref[...]).astype(BF16)

    def proj(gi):
        return jnp.dot(h, w_ref[:, gi * GROUP:(gi + 1) * GROUP], preferred_element_type=F32)

    c = atab_ref[:, 0:LANES]
    s_up = atab_ref[:, LANES:2 * LANES]
    s_dn = atab_ref[:, 2 * LANES:3 * LANES]
    for gi, out_ref, scale in ((0, aq_ref, (A_QK_DIM ** -0.5) * math.log2(math.e)),
                               (1, ak_ref, None)):
        y = proj(gi)
        for hh in range(A_HEADS):
            sl = slice(hh * LANES, (hh + 1) * LANES)
            rot = _rope_block(y[:, sl], c, s_up, s_dn, ROPE_DIM // 2)
            out_ref[:, sl] = (rot if scale is None else rot * scale).astype(BF16)
    av_ref[...] = proj(2).astype(BF16)
    az_ref[...] = _silu(proj(3)).astype(BF16)

    u = proj(4) * proj(6)

    @pl.when(i % tiles_per_seq == 0)
    def _():
        ubuf_ref[0:8, :] = jnp.zeros((8, GROUP), F32)

    ubuf_ref[8:8 + tm, :] = u
    conv = (cw_ref[0:1, :] * ubuf_ref[6:6 + tm, :] + cw_ref[1:2, :] * ubuf_ref[7:7 + tm, :]
            + cw_ref[2:3, :] * u)
    ubuf_ref[0:8, :] = u[tm - 8:tm, :]
    c_ref[...] = (proj(5) * conv * _silu(proj(7))).astype(BF16)

    y = proj(8)
    c = rtab_ref[:, 0:LANES]
    s_up = rtab_ref[:, LANES:2 * LANES]
    s_dn = rtab_ref[:, 2 * LANES:3 * LANES]
    lane = lax.broadcasted_iota(jnp.int32, (tm, LANES), 1)
    low = lane < R_QK_DIM
    for hh in range(R_HEADS):
        sl = slice(hh * LANES, (hh + 1) * LANES)
        rot = _rope_block(y[:, sl], c, s_up, s_dn, R_QK_DIM // 2)
        rq_ref[:, sl] = jnp.where(low, rot, 0.0).astype(BF16)
        rk_ref[:, sl] = jnp.where(low, pltpu.roll(rot, R_QK_DIM, 1), 0.0).astype(BF16)
    rv_ref[...] = proj(9).astype(BF16)
    rz_ref[...] = _silu(proj(10)).astype(BF16)


def _inproj(x2, norm_g, w1, atab, rtab, conv_w, seq):
    n, d = x2.shape
    tm = PROJ_ROWS
    tiles_per_seq = seq // tm
    row = lambda i: (i, 0)
    pos = lambda i: (i % tiles_per_seq, 0)
    const = lambda i: (0, 0)
    out_sds = jax.ShapeDtypeStruct((n, GROUP), BF16)
    out_spec = pl.BlockSpec((tm, GROUP), row)
    return pl.pallas_call(
        functools.partial(_inproj_kernel, tiles_per_seq=tiles_per_seq),
        grid=(n // tm,),
        in_specs=[
            pl.BlockSpec((tm, d), row),
            pl.BlockSpec((1, d), const),
            pl.BlockSpec(w1.shape, const),
            pl.BlockSpec((tm, atab.shape[1]), pos),
            pl.BlockSpec((tm, rtab.shape[1]), pos),
            pl.BlockSpec(conv_w.shape, const),
        ],
        out_specs=[out_spec] * 9,
        out_shape=[out_sds] * 9,
        scratch_shapes=[pltpu.VMEM((tm + 8, GROUP), F32)],
        compiler_params=pltpu.CompilerParams(
            dimension_semantics=("arbitrary",), vmem_limit_bytes=VMEM_LIMIT),
        name="inproj",
    )(x2, norm_g, w1, atab, rtab, conv_w)


def _attn_kernel(lam_ref, g_ref, q_ref, k_ref, v_ref, z_ref, o_ref,
                 qs_ref, m_ref, l_ref, acc_ref, *, lam_init):
    i = pl.program_id(2)
    tq = q_ref.shape[0]
    half = tq // 2
    lane = lax.broadcasted_iota(jnp.int32, (half, LANES), 1)
    zero = jnp.zeros((half, LANES), BF16)
    for part in range(2):
        qp = q_ref[part * half:(part + 1) * half, :]
        qs_ref[(2 * part) * half:(2 * part + 1) * half, :] = jnp.where(lane < A_QK_DIM, qp, zero)
        qs_ref[(2 * part + 1) * half:(2 * part + 2) * half, :] = jnp.where(
            lane >= A_QK_DIM, qp, zero)

    def scores(j, width, r0=0, nrows=2 * tq, diag_shift=None):
        keys = pl.ds(pl.multiple_of(j * tq, tq), width)
        s = lax.dot_general(qs_ref[r0:r0 + nrows, :], k_ref[keys, :], (((1,), (1,)), ((), ())),
                            preferred_element_type=F32)
        if diag_shift is not None:
            r = lax.broadcasted_iota(jnp.int32, s.shape, 0)
            cidx = lax.broadcasted_iota(jnp.int32, s.shape, 1)
            s = jnp.where(cidx <= diag_shift + jnp.where(r >= half, r - half, r), s, NEG_INF)
        return s

    def update(s, j, r0=0, first=False):
        nrows, width = s.shape
        keys = pl.ds(pl.multiple_of(j * tq, tq), width)
        rsl = slice(r0, r0 + nrows)
        m_new = jnp.broadcast_to(jnp.max(s, axis=1, keepdims=True), (nrows, LANES))
        if not first:
            m_prev = m_ref[rsl, :]
            m_new = jnp.maximum(m_prev, m_new)
            alpha = jnp.exp2(m_prev - m_new)
        m_ref[rsl, :] = m_new
        p = jnp.concatenate(
            [jnp.exp2(s[:, c * LANES:(c + 1) * LANES] - m_new).astype(BF16)
             for c in range(width // LANES)], axis=1)
        vs = v_ref[keys, :]
        v1 = jnp.concatenate([vs, jnp.ones_like(vs)], axis=1)
        for part in range(2):
            psl = slice(part * nrows // 2, (part + 1) * nrows // 2)
            osl = slice(r0 + part * nrows // 2, r0 + (part + 1) * nrows // 2)
            res = jnp.dot(p[psl], v1, preferred_element_type=F32)
            if first:
                acc_ref[osl, :] = res[:, :LANES]
                l_ref[osl, :] = res[:, LANES:]
            else:
                acc_ref[osl, :] = alpha[psl] * acc_ref[osl, :] + res[:, :LANES]
                l_ref[osl, :] = alpha[psl] * l_ref[osl, :] + res[:, LANES:]

    s_lo = scores(i, half, r0=0, nrows=tq, diag_shift=0)
    s_hi = scores(i, tq, r0=tq, nrows=tq, diag_shift=half)
    update(s_lo, i, r0=0, first=True)
    update(s_hi, i, r0=tq, first=True)

    def wide(kk, carry):
        update(scores(2 * kk, 2 * tq), 2 * kk)
        return carry

    lax.fori_loop(0, i // 2, wide, 0)

    @pl.when(i % 2 == 1)
    def _():
        update(scores(i - 1, tq), i - 1)

    o = acc_ref[...] / l_ref[...]
    al = lam_ref[...]
    lam = (jnp.exp(jnp.sum(al[0:1] * al[1:2], axis=1, keepdims=True))
           - jnp.exp(jnp.sum(al[2:3] * al[3:4], axis=1, keepdims=True)) + lam_init)
    d = jnp.concatenate([o[0:half] - lam * o[half:tq],
                         o[tq:tq + half] - lam * o[tq + half:2 * tq]], axis=0)
    d = _rms(d) * g_ref[...] * (1.0 - lam_init)
    o_ref[...] = (d * z_ref[...].astype(F32)).astype(BF16)


def _attention(aq, ak, av, az, attn_lambda, subln_g, lam_init):
    b, s, _ = aq.shape
    tq = ATT_TILE
    tile = pl.BlockSpec((None, tq, LANES), lambda bi, hi, qi: (bi, qi, hi))
    full = pl.BlockSpec((None, s, LANES), lambda bi, hi, qi: (bi, 0, hi))
    const = lambda bi, hi, qi: (0, 0)
    return pl.pallas_call(
        functools.partial(_attn_kernel, lam_init=lam_init),
        grid=(b, A_HEADS, s // tq),
        in_specs=[
            pl.BlockSpec(attn_lambda.shape, const),
            pl.BlockSpec(subln_g.shape, const),
            tile, full, full, tile,
        ],
        out_specs=tile,
        out_shape=jax.ShapeDtypeStruct(aq.shape, BF16),
        scratch_shapes=[pltpu.VMEM((2 * tq, LANES), BF16),
                        pltpu.VMEM((2 * tq, LANES), F32), pltpu.VMEM((2 * tq, LANES), F32),
                        pltpu.VMEM((2 * tq, LANES), F32)],
        compiler_params=pltpu.CompilerParams(
            dimension_semantics=("arbitrary", "arbitrary", "arbitrary"),
            vmem_limit_bytes=VMEM_LIMIT),
        name="diff_attention",
    )(attn_lambda, subln_g, aq, ak, av, az)


def _ret_kernel(q_ref, k_ref, v_ref, z_ref, dm_ref, xi_ref, zeta_ref, dec_ref, o_ref, st_ref):
    @pl.when(pl.program_id(1) == 0)
    def _():
        st_ref[...] = jnp.zeros(st_ref.shape, F32)

    chunk = dm_ref.shape[1]
    for hh in range(R_HEADS):
        sl = slice(hh * LANES, (hh + 1) * LANES)
        for cc in range(q_ref.shape[0] // chunk):
            rows = slice(cc * chunk, (cc + 1) * chunk)
            q = q_ref[rows, sl]
            k = k_ref[rows, sl]
            v = v_ref[rows, sl]
            s = lax.dot_general(q, k, (((1,), (1,)), ((), ())), preferred_element_type=F32)
            inner = jnp.dot((s * dm_ref[hh]).astype(BF16), v, preferred_element_type=F32)
            st = st_ref[hh]
            cross = jnp.dot(q, st.astype(BF16), preferred_element_type=F32) * xi_ref[hh]
            o = _rms(inner + cross)
            o_ref[rows, sl] = (o * z_ref[rows, sl].astype(F32)).astype(BF16)
            kz = (k.astype(F32) * zeta_ref[hh]).astype(BF16)
            kv = lax.dot_general(kz, v, (((0,), (0,)), ((), ())), preferred_element_type=F32)
            st_ref[hh] = dec_ref[hh] * st + kv


def _retention(rq, rk, rv, rz, dmask, xi, zeta, dec):
    b, s, w = rq.shape
    c = RET_ROWS
    tile = pl.BlockSpec((None, c, w), lambda bi, ti: (bi, ti, 0))
    const3 = lambda bi, ti: (0, 0, 0)
    return pl.pallas_call(
        _ret_kernel,
        grid=(b, s // c),
        in_specs=[tile, tile, tile, tile,
                  pl.BlockSpec(dmask.shape, const3), pl.BlockSpec(xi.shape, const3),
                  pl.BlockSpec(zeta.shape, const3), pl.BlockSpec(dec.shape, const3)],
        out_specs=tile,
        out_shape=jax.ShapeDtypeStruct(rq.shape, BF16),
        scratch_shapes=[pltpu.VMEM((R_HEADS, LANES, LANES), F32)],
        compiler_params=pltpu.CompilerParams(
            dimension_semantics=("arbitrary", "arbitrary"), vmem_limit_bytes=VMEM_LIMIT),
        name="retention",
    )(rq, rk, rv, rz, dmask, xi, zeta, dec)


def _merge_kernel(x_ref, a_ref, c_ref, r_ref, g_ref, wg_ref, wb_ref, wo_ref, fg_ref, o_ref,
                  *, final_norm):
    x = x_ref[...]
    d = x.shape[1]
    h = (_rms(x) * g_ref[...]).astype(BF16)
    merged = None
    for bi, br_ref in enumerate((a_ref, c_ref, r_ref)):
        t = jnp.dot(h, wg_ref[:, bi * d:(bi + 1) * d], preferred_element_type=F32)
        gate = 1.0 / (1.0 + jnp.exp(-t))
        contrib = gate * jnp.dot(br_ref[...], wb_ref[bi], preferred_element_type=F32)
        merged = contrib if merged is None else merged + contrib
    y = x + jnp.dot(merged.astype(BF16), wo_ref[...], preferred_element_type=F32)
    if final_norm:
        y = _rms(y) * fg_ref[...]
    o_ref[...] = y


def _merge(x2, a, c, r, norm_g, wg, wb, wo, final_g, final_norm):
    n, d = x2.shape
    tm = MERGE_ROWS
    row = lambda i: (i, 0)
    const = lambda i: (0, 0)
    br = pl.BlockSpec((tm, GROUP), row)
    return pl.pallas_call(
        functools.partial(_merge_kernel, final_norm=final_norm),
        grid=(n // tm,),
        in_specs=[pl.BlockSpec((tm, d), row), br, br, br,
                  pl.BlockSpec((1, d), const),
                  pl.BlockSpec(wg.shape, const),
                  pl.BlockSpec(wb.shape, lambda i: (0, 0, 0)),
                  pl.BlockSpec(wo.shape, const),
                  pl.BlockSpec((1, d), const)],
        out_specs=pl.BlockSpec((tm, d), row),
        out_shape=jax.ShapeDtypeStruct((n, d), F32),
        compiler_params=pltpu.CompilerParams(
            dimension_semantics=("arbitrary",), vmem_limit_bytes=VMEM_LIMIT),
        name="merge_outproj",
    )(x2, a, c, r, norm_g, wg, wb, wo, final_g)


def _attn_rope_table(seq):
    pos = jnp.arange(seq, dtype=F32)
    inv = ROPE_THETA ** (-jnp.arange(0, ROPE_DIM, 2, dtype=F32) / ROPE_DIM)
    ang = pos[:, None] * inv[None, :]
    half = ROPE_DIM // 2
    cos = jnp.tile(jnp.cos(ang), (1, LANES // half))
    sin = jnp.tile(jnp.sin(ang), (1, LANES // half))
    dd = (jnp.arange(LANES) % A_QK_DIM)[None, :]
    c = jnp.where(dd < ROPE_DIM, cos, 1.0)
    s_up = jnp.where(dd < half, -sin, 0.0)
    s_dn = jnp.where((dd >= half) & (dd < ROPE_DIM), sin, 0.0)
    return jnp.concatenate([c, s_up, s_dn], axis=1).astype(F32)


def _ret_rope_table(seq):
    pos = jnp.arange(seq, dtype=F32)
    inv = 1.0 / (RET_THETA ** jnp.linspace(0.0, 1.0, R_QK_DIM // 2, dtype=F32))
    ang = pos[:, None] * inv[None, :]
    half = R_QK_DIM // 2
    c = jnp.tile(jnp.cos(ang), (1, LANES // half))
    sin = jnp.tile(jnp.sin(ang), (1, LANES // half))
    dd = (jnp.arange(LANES) % R_QK_DIM)[None, :]
    s_up = jnp.where(dd < half, -sin, 0.0)
    s_dn = jnp.where(dd >= half, sin, 0.0)
    scale = jnp.where(jnp.arange(LANES) < R_QK_DIM, 1.0, R_QK_DIM ** -0.5)[None, :]
    return jnp.concatenate([c * scale, s_up * scale, s_dn * scale], axis=1).astype(F32)


def _ret_decay_tables(chunk):
    log_g = jnp.log(1.0 - 2.0 ** (-5.0 - jnp.arange(R_HEADS, dtype=F32)))
    idx = jnp.arange(chunk, dtype=F32)
    diff = idx[:, None] - idx[None, :]
    dmask = jnp.where(diff >= 0,
                      jnp.exp(jnp.where(diff >= 0, diff, 0.0)[None] * log_g[:, None, None]),
                      0.0)
    ones = jnp.ones((1, 1, LANES), F32)
    xi = jnp.exp((idx + 1.0)[None, :] * log_g[:, None])[:, :, None] * ones
    zeta = jnp.exp((chunk - 1 - idx)[None, :] * log_g[:, None])[:, :, None] * ones
    dec = jnp.exp(chunk * log_g)[:, None, None] * ones
    return dmask.astype(F32), xi.astype(F32), zeta.astype(F32), dec.astype(F32)


def _split_w_in(w_in_l):
    d = w_in_l.shape[0]
    g = GROUP
    mix = w_in_l[:, :8 * g]
    rq = w_in_l[:, 8 * g:8 * g + g // 2].reshape(d, R_HEADS, R_QK_DIM)
    rk = w_in_l[:, 8 * g + g // 2:9 * g].reshape(d, R_HEADS, R_QK_DIM)
    rqk = jnp.concatenate([rq, rk], axis=2).reshape(d, g)
    rest = w_in_l[:, 9 * g:11 * g]
    w1 = jnp.concatenate([mix, rqk, rest], axis=1).astype(BF16)
    wg = w_in_l[:, 11 * g:].astype(BF16)
    return w1, wg


def kernel(x, norm_g, w_in, attn_lambda, attn_subln_g, conv_w, w_branch, w_out, final_norm_g):
    b, s, d = x.shape
    depth = w_in.shape[0]
    x2 = x.reshape(b * s, d)
    atab = _attn_rope_table(s)
    rtab = _ret_rope_table(s)
    dmask, xi, zeta, dec = _ret_decay_tables(RET_CHUNK)
    final_g = final_norm_g.reshape(1, d)
    for layer in range(depth):
        lam_init = 0.8 - 0.6 * math.exp(-0.3 * layer)
        w1, wg = _split_w_in(w_in[layer])
        g = norm_g[layer].reshape(1, d)
        aq, ak, av, az, c, rq, rk, rv, rz = _inproj(x2, g, w1, atab, rtab, conv_w[layer], s)
        sh = (b, s, GROUP)
        a = _attention(aq.reshape(sh), ak.reshape(sh), av.reshape(sh), az.reshape(sh),
                       attn_lambda[layer], attn_subln_g[layer].reshape(1, LANES), lam_init)
        r = _retention(rq.reshape(sh), rk.reshape(sh), rv.reshape(sh), rz.reshape(sh),
                       dmask, xi, zeta, dec)
        x2 = _merge(x2, a.reshape(b * s, GROUP), c, r.reshape(b * s, GROUP), g, wg,
                    w_branch[layer].astype(BF16), w_out[layer].astype(BF16), final_g,
                    layer == depth - 1)
    return x2.reshape(b, s, d)
```

```python
import functools
import math

import jax
import jax.numpy as jnp
from jax import lax
from jax.experimental import pallas as pl
from jax.experimental.pallas import tpu as pltpu

F32 = jnp.float32
BF16 = jnp.bfloat16

A_HEADS = 4
A_QK_DIM = 64
ROPE_THETA = 500000.0
ROPE_DIM = A_QK_DIM // 4
NEG_INF = -1e30
CONV_WIDTH = 3
R_HEADS = 4
R_QK_DIM = 64
RET_THETA = 10000.0
N_BRANCH = 3
EPS = 1e-6

LANES = 128
GROUP = 512
VMEM_LIMIT = 56 * 1024 * 1024

PROJ_ROWS = 512
MERGE_ROWS = 512
ATT_TILE = 1024
RET_ROWS = 2048
RET_CHUNK = 256


def _silu(t):
    return t * (1.0 / (1.0 + jnp.exp(-t)))


def _rms(x):
    return x * lax.rsqrt(jnp.mean(x * x, axis=-1, keepdims=True) + EPS)


def _rope_block(blk, c, s_up, s_dn, shift):
    return (blk * c + pltpu.roll(blk, LANES - shift, 1) * s_up
            + pltpu.roll(blk, shift, 1) * s_dn)


def _inproj_kernel(x_ref, g_ref, w_ref, atab_ref, rtab_ref, cw_ref,
                   aq_ref, ak_ref, av_ref, az_ref, c_ref,
                   rq_ref, rk_ref, rv_ref, rz_ref, ubuf_ref, *, tiles_per_seq):
    i = pl.program_id(0)
    tm = x_ref.shape[0]
    h = (_rms(x_ref[...]) * g_ref[...]).astype(BF16)

    def proj(gi):
        return jnp.dot(h, w_ref[:, gi * GROUP:(gi + 1) * GROUP], preferred_element_type=F32)

    c = atab_ref[:, 0:LANES]
    s_up = atab_ref[:, LANES:2 * LANES]
    s_dn = atab_ref[:, 2 * LANES:3 * LANES]
    for gi, out_ref, scale in ((0, aq_ref, (A_QK_DIM ** -0.5) * math.log2(math.e)),
                               (1, ak_ref, None)):
        y = proj(gi)
        for hh in range(A_HEADS):
            sl = slice(hh * LANES, (hh + 1) * LANES)
            rot = _rope_block(y[:, sl], c, s_up, s_dn, ROPE_DIM // 2)
            out_ref[:, sl] = (rot if scale is None else rot * scale).astype(BF16)
    av_ref[...] = proj(2).astype(BF16)
    az_ref[...] = _silu(proj(3)).astype(BF16)

    u = proj(4) * proj(6)

    @pl.when(i % tiles_per_seq == 0)
    def _():
        ubuf_ref[0:8, :] = jnp.zeros((8, GROUP), F32)

    ubuf_ref[8:8 + tm, :] = u
    conv = (cw_ref[0:1, :] * ubuf_ref[6:6 + tm, :] + cw_ref[1:2, :] * ubuf_ref[7:7 + tm, :]
            + cw_ref[2:3, :] * u)
    ubuf_ref[0:8, :] = u[tm - 8:tm, :]
    c_ref[...] = (proj(5) * conv * _silu(proj(7))).astype(BF16)

    y = proj(8)
    c = rtab_ref[:, 0:LANES]
    s_up = rtab_ref[:, LANES:2 * LANES]
    s_dn = rtab_ref[:, 2 * LANES:3 * LANES]
    lane = lax.broadcasted_iota(jnp.int32, (tm, LANES), 1)
    low = lane < R_QK_DIM
    for hh in range(R_HEADS):
        sl = slice(hh * LANES, (hh + 1) * LANES)
        rot = _rope_block(y[:, sl], c, s_up, s_dn, R_QK_DIM // 2)
        rq_ref[:, sl] = jnp.where(low, rot, 0.0).astype(BF16)
        rk_ref[:, sl] = jnp.where(low, pltpu.roll(rot, R_QK_DIM, 1), 0.0).astype(BF16)
    rv_ref[...] = proj(9).astype(BF16)
    rz_ref[...] = _silu(proj(10)).astype(BF16)


def _inproj(x2, norm_g, w1, atab, rtab, conv_w, seq):
    n, d = x2.shape
    tm = PROJ_ROWS
    tiles_per_seq = seq // tm
    row = lambda i: (i, 0)
    pos = lambda i: (i % tiles_per_seq, 0)
    const = lambda i: (0, 0)
    out_sds = jax.ShapeDtypeStruct((n, GROUP), BF16)
    out_spec = pl.BlockSpec((tm, GROUP), row)
    return pl.pallas_call(
        functools.partial(_inproj_kernel, tiles_per_seq=tiles_per_seq),
        grid=(n // tm,),
        in_specs=[
            pl.BlockSpec((tm, d), row),
            pl.BlockSpec((1, d), const),
            pl.BlockSpec(w1.shape, const),
            pl.BlockSpec((tm, atab.shape[1]), pos),
            pl.BlockSpec((tm, rtab.shape[1]), pos),
            pl.BlockSpec(conv_w.shape, const),
        ],
        out_specs=[out_spec] * 9,
        out_shape=[out_sds] * 9,
        scratch_shapes=[pltpu.VMEM((tm + 8, GROUP), F32)],
        compiler_params=pltpu.CompilerParams(
            dimension_semantics=("arbitrary",), vmem_limit_bytes=VMEM_LIMIT),
        name="inproj",
    )(x2, norm_g, w1, atab, rtab, conv_w)


def _attn_kernel(lam_ref, g_ref, q_ref, k_ref, v_ref, z_ref, o_ref,
                 qs_ref, m_ref, l_ref, acc_ref, *, lam_init):
    i = pl.program_id(2)
    tq = q_ref.shape[0]
    half = tq // 2
    lane = lax.broadcasted_iota(jnp.int32, (half, LANES), 1)
    zero = jnp.zeros((half, LANES), BF16)
    for part in range(2):
        qp = q_ref[part * half:(part + 1) * half, :]
        qs_ref[(2 * part) * half:(2 * part + 1) * half, :] = jnp.where(lane < A_QK_DIM, qp, zero)
        qs_ref[(2 * part + 1) * half:(2 * part + 2) * half, :] = jnp.where(
            lane >= A_QK_DIM, qp, zero)

    def scores(j, width, r0=0, nrows=2 * tq, diag_shift=None):
        keys = pl.ds(pl.multiple_of(j * tq, tq), width)
        s = lax.dot_general(qs_ref[r0:r0 + nrows, :], k_ref[keys, :], (((1,), (1,)), ((), ())),
                            preferred_element_type=F32)
        if diag_shift is not None:
            r = lax.broadcasted_iota(jnp.int32, s.shape, 0)
            cidx = lax.broadcasted_iota(jnp.int32, s.shape, 1)
            s = jnp.where(cidx <= diag_shift + jnp.where(r >= half, r - half, r), s, NEG_INF)
        return s

    def update(s, j, r0=0, first=False):
        nrows, width = s.shape
        keys = pl.ds(pl.multiple_of(j * tq, tq), width)
        rsl = slice(r0, r0 + nrows)
        m_new = jnp.broadcast_to(jnp.max(s, axis=1, keepdims=True), (nrows, LANES))
        if not first:
            m_prev = m_ref[rsl, :]
            m_new = jnp.maximum(m_prev, m_new)
            alpha = jnp.exp2(m_prev - m_new)
        m_ref[rsl, :] = m_new
        p = jnp.concatenate(
            [jnp.exp2(s[:, c * LANES:(c + 1) * LANES] - m_new).astype(BF16)
             for c in range(width // LANES)], axis=1)
        vs = v_ref[keys, :]
        v1 = jnp.concatenate([vs, jnp.ones_like(vs)], axis=1)
        for part in range(2):
            psl = slice(part * nrows // 2, (part + 1) * nrows // 2)
            osl = slice(r0 + part * nrows // 2, r0 + (part + 1) * nrows // 2)
            res = jnp.dot(p[psl], v1, preferred_element_type=F32)
            if first:
                acc_ref[osl, :] = res[:, :LANES]
                l_ref[osl, :] = res[:, LANES:]
            else:
                acc_ref[osl, :] = alpha[psl] * acc_ref[osl, :] + res[:, :LANES]
                l_ref[osl, :] = alpha[psl] * l_ref[osl, :] + res[:, LANES:]

    s_lo = scores(i, half, r0=0, nrows=tq, diag_shift=0)
    s_hi = scores(i, tq, r0=tq, nrows=tq, diag_shift=half)
    update(s_lo, i, r0=0, first=True)
    update(s_hi, i, r0=tq, first=True)

    def wide(kk, carry):
        update(scores(2 * kk, 2 * tq), 2 * kk)
        return carry

    lax.fori_loop(0, i // 2, wide, 0)

    @pl.when(i % 2 == 1)
    def _():
        update(scores(i - 1, tq), i - 1)

    o = acc_ref[...] / l_ref[...]
    al = lam_ref[...]
    lam = (jnp.exp(jnp.sum(al[0:1] * al[1:2], axis=1, keepdims=True))
           - jnp.exp(jnp.sum(al[2:3] * al[3:4], axis=1, keepdims=True)) + lam_init)
    d = jnp.concatenate([o[0:half] - lam * o[half:tq],
                         o[tq:tq + half] - lam * o[tq + half:2 * tq]], axis=0)
    d = _rms(d) * g_ref[...] * (1.0 - lam_init)
    o_ref[...] = (d * z_ref[...].astype(F32)).astype(BF16)


def _attention(aq, ak, av, az, attn_lambda, subln_g, lam_init):
    b, s, _ = aq.shape
    tq = ATT_TILE
    tile = pl.BlockSpec((None, tq, LANES), lambda bi, hi, qi: (bi, qi, hi))
    full = pl.BlockSpec((None, s, LANES), lambda bi, hi, qi: (bi, 0, hi))
    const = lambda bi, hi, qi: (0, 0)
    return pl.pallas_call(
        functools.partial(_attn_kernel, lam_init=lam_init),
        grid=(b, A_HEADS, s // tq),
        in_specs=[
            pl.BlockSpec(attn_lambda.shape, const),
            pl.BlockSpec(subln_g.shape, const),
            tile, full, full, tile,
        ],
        out_specs=tile,
        out_shape=jax.ShapeDtypeStruct(aq.shape, BF16),
        scratch_shapes=[pltpu.VMEM((2 * tq, LANES), BF16),
                        pltpu.VMEM((2 * tq, LANES), F32), pltpu.VMEM((2 * tq, LANES), F32),
                        pltpu.VMEM((2 * tq, LANES), F32)],
        compiler_params=pltpu.CompilerParams(
            dimension_semantics=("arbitrary", "arbitrary", "arbitrary"),
            vmem_limit_bytes=VMEM_LIMIT),
        name="diff_attention",
    )(attn_lambda, subln_g, aq, ak, av, az)


def _ret_kernel(q_ref, k_ref, v_ref, z_ref, dm_ref, xi_ref, zeta_ref, dec_ref, o_ref, st_ref):
    @pl.when(pl.program_id(1) == 0)
    def _():
        st_ref[...] = jnp.zeros(st_ref.shape, F32)

    chunk = dm_ref.shape[1]
    for hh in range(R_HEADS):
        sl = slice(hh * LANES, (hh + 1) * LANES)
        for cc in range(q_ref.shape[0] // chunk):
            rows = slice(cc * chunk, (cc + 1) * chunk)
            q = q_ref[rows, sl]
            k = k_ref[rows, sl]
            v = v_ref[rows, sl]
            s = lax.dot_general(q, k, (((1,), (1,)), ((), ())), preferred_element_type=F32)
            inner = jnp.dot((s * dm_ref[hh]).astype(BF16), v, preferred_element_type=F32)
            st = st_ref[hh]
            cross = jnp.dot(q, st.astype(BF16), preferred_element_type=F32) * xi_ref[hh]
            o = _rms(inner + cross)
            o_ref[rows, sl] = (o * z_ref[rows, sl].astype(F32)).astype(BF16)
            kz = (k.astype(F32) * zeta_ref[hh]).astype(BF16)
            kv = lax.dot_general(kz, v, (((0,), (0,)), ((), ())), preferred_element_type=F32)
            st_ref[hh] = dec_ref[hh] * st + kv


def _retention(rq, rk, rv, rz, dmask, xi, zeta, dec):
    b, s, w = rq.shape
    c = RET_ROWS
    tile = pl.BlockSpec((None, c, w), lambda bi, ti: (bi, ti, 0))
    const3 = lambda bi, ti: (0, 0, 0)
    return pl.pallas_call(
        _ret_kernel,
        grid=(b, s // c),
        in_specs=[tile, tile, tile, tile,
                  pl.BlockSpec(dmask.shape, const3), pl.BlockSpec(xi.shape, const3),
                  pl.BlockSpec(zeta.shape, const3), pl.BlockSpec(dec.shape, const3)],
        out_specs=tile,
        out_shape=jax.ShapeDtypeStruct(rq.shape, BF16),
        scratch_shapes=[pltpu.VMEM((R_HEADS, LANES, LANES), F32)],
        compiler_params=pltpu.CompilerParams(
            dimension_semantics=("arbitrary", "arbitrary"), vmem_limit_bytes=VMEM_LIMIT),
        name="retention",
    )(rq, rk, rv, rz, dmask, xi, zeta, dec)


def _merge_kernel(x_ref, a_ref, c_ref, r_ref, g_ref, wg_ref, wb_ref, wo_ref, fg_ref, o_ref,
                  *, final_norm):
    x = x_ref[...]
    d = x.shape[1]
    h = (_rms(x) * g_ref[...]).astype(BF16)
    merged = None
    for bi, br_ref in enumerate((a_ref, c_ref, r_ref)):
        t = jnp.dot(h, wg_ref[:, bi * d:(bi + 1) * d], preferred_element_type=F32)
        gate = 1.0 / (1.0 + jnp.exp(-t))
        contrib = gate * jnp.dot(br_ref[...], wb_ref[bi], preferred_element_type=F32)
        merged = contrib if merged is None else merged + contrib
    y = x + jnp.dot(merged.astype(BF16), wo_ref[...], preferred_element_type=F32)
    if final_norm:
        y = _rms(y) * fg_ref[...]
    o_ref[...] = y


def _merge(x2, a, c, r, norm_g, wg, wb, wo, final_g, final_norm):
    n, d = x2.shape
    tm = MERGE_ROWS
    row = lambda i: (i, 0)
    const = lambda i: (0, 0)
    br = pl.BlockSpec((tm, GROUP), row)
    return pl.pallas_call(
        functools.partial(_merge_kernel, final_norm=final_norm),
        grid=(n // tm,),
        in_specs=[pl.BlockSpec((tm, d), row), br, br, br,
                  pl.BlockSpec((1, d), const),
                  pl.BlockSpec(wg.shape, const),
                  pl.BlockSpec(wb.shape, lambda i: (0, 0, 0)),
                  pl.BlockSpec(wo.shape, const),
                  pl.BlockSpec((1, d), const)],
        out_specs=pl.BlockSpec((tm, d), row),
        out_shape=jax.ShapeDtypeStruct((n, d), F32),
        compiler_params=pltpu.CompilerParams(
            dimension_semantics=("arbitrary",), vmem_limit_bytes=VMEM_LIMIT),
        name="merge_outproj",
    )(x2, a, c, r, norm_g, wg, wb, wo, final_g)


def _attn_rope_table(seq):
    pos = jnp.arange(seq, dtype=F32)
    inv = ROPE_THETA ** (-jnp.arange(0, ROPE_DIM, 2, dtype=F32) / ROPE_DIM)
    ang = pos[:, None] * inv[None, :]
    half = ROPE_DIM // 2
    cos = jnp.tile(jnp.cos(ang), (1, LANES // half))
    sin = jnp.tile(jnp.sin(ang), (1, LANES // half))
    dd = (jnp.arange(LANES) % A_QK_DIM)[None, :]
    c = jnp.where(dd < ROPE_DIM, cos, 1.0)
    s_up = jnp.where(dd < half, -sin, 0.0)
    s_dn = jnp.where((dd >= half) & (dd < ROPE_DIM), sin, 0.0)
    return jnp.concatenate([c, s_up, s_dn], axis=1).astype(F32)


def _ret_rope_table(seq):
    pos = jnp.arange(seq, dtype=F32)
    inv = 1.0 / (RET_THETA ** jnp.linspace(0.0, 1.0, R_QK_DIM // 2, dtype=F32))
    ang = pos[:, None] * inv[None, :]
    half = R_QK_DIM // 2
    c = jnp.tile(jnp.cos(ang), (1, LANES // half))
    sin = jnp.tile(jnp.sin(ang), (1, LANES // half))
    dd = (jnp.arange(LANES) % R_QK_DIM)[None, :]
    s_up = jnp.where(dd < half, -sin, 0.0)
    s_dn = jnp.where(dd >= half, sin, 0.0)
    scale = jnp.where(jnp.arange(LANES) < R_QK_DIM, 1.0, R_QK_DIM ** -0.5)[None, :]
    return jnp.concatenate([c * scale, s_up * scale, s_dn * scale], axis=1).astype(F32)


def _ret_decay_tables(chunk):
    log_g = jnp.log(1.0 - 2.0 ** (-5.0 - jnp.arange(R_HEADS, dtype=F32)))
    idx = jnp.arange(chunk, dtype=F32)
    diff = idx[:, None] - idx[None, :]
    dmask = jnp.where(diff >= 0,
                      jnp.exp(jnp.where(diff >= 0, diff, 0.0)[None] * log_g[:, None, None]),
                      0.0)
    ones = jnp.ones((1, 1, LANES), F32)
    xi = jnp.exp((idx + 1.0)[None, :] * log_g[:, None])[:, :, None] * ones
    zeta = jnp.exp((chunk - 1 - idx)[None, :] * log_g[:, None])[:, :, None] * ones
    dec = jnp.exp(chunk * log_g)[:, None, None] * ones
    return dmask.astype(F32), xi.astype(F32), zeta.astype(F32), dec.astype(F32)


def _split_w_in(w_in_l):
    d = w_in_l.shape[0]
    g = GROUP
    mix = w_in_l[:, :8 * g]
    rq = w_in_l[:, 8 * g:8 * g + g // 2].reshape(d, R_HEADS, R_QK_DIM)
    rk = w_in_l[:, 8 * g + g // 2:9 * g].reshape(d, R_HEADS, R_QK_DIM)
    rqk = jnp.concatenate([rq, rk], axis=2).reshape(d, g)
    rest = w_in_l[:, 9 * g:11 * g]
    w1 = jnp.concatenate([mix, rqk, rest], axis=1).astype(BF16)
    wg = w_in_l[:, 11 * g:].astype(BF16)
    return w1, wg


def kernel(x, norm_g, w_in, attn_lambda, attn_subln_g, conv_w, w_branch, w_out, final_norm_g):
    b, s, d = x.shape
    depth = w_in.shape[0]
    assert s % PROJ_ROWS == 0 and s % MERGE_ROWS == 0 and s % ATT_TILE == 0
    assert s % RET_ROWS == 0 and RET_ROWS % RET_CHUNK == 0 and ATT_TILE % (2 * LANES) == 0
    assert w_in.shape[2] == (11 + N_BRANCH * d // GROUP) * GROUP and d % LANES == 0
    x2 = x.reshape(b * s, d)
    atab = _attn_rope_table(s)
    rtab = _ret_rope_table(s)
    dmask, xi, zeta, dec = _ret_decay_tables(RET_CHUNK)
    final_g = final_norm_g.reshape(1, d)
    for layer in range(depth):
        lam_init = 0.8 - 0.6 * math.exp(-0.3 * layer)
        w1, wg = _split_w_in(w_in[layer])
        g = norm_g[layer].reshape(1, d)
        aq, ak, av, az, c, rq, rk, rv, rz = _inproj(x2, g, w1, atab, rtab, conv_w[layer], s)
        sh = (b, s, GROUP)
        a = _attention(aq.reshape(sh), ak.reshape(sh), av.reshape(sh), az.reshape(sh),
                       attn_lambda[layer], attn_subln_g[layer].reshape(1, LANES), lam_init)
        r = _retention(rq.reshape(sh), rk.reshape(sh), rv.reshape(sh), rz.reshape(sh),
                       dmask, xi, zeta, dec)
        x2 = _merge(x2, a.reshape(b * s, GROUP), c, r.reshape(b * s, GROUP), g, wg,
                    w_branch[layer].astype(BF16), w_out[layer].astype(BF16), final_g,
                    layer == depth - 1)
    return x2.reshape(b, s, d)
```

```python
import functools
import math

import jax
import jax.numpy as jnp
from jax import lax
from jax.experimental import pallas as pl
from jax.experimental.pallas import tpu as pltpu

F32 = jnp.float32
BF16 = jnp.bfloat16

A_HEADS = 4
A_QK_DIM = 64
ROPE_THETA = 500000.0
ROPE_DIM = A_QK_DIM // 4
NEG_INF = -1e30
CONV_WIDTH = 3
R_HEADS = 4
R_QK_DIM = 64
RET_THETA = 10000.0
N_BRANCH = 3
EPS = 1e-6

LANES = 128
GROUP = 512
VMEM_LIMIT = 56 * 1024 * 1024

PROJ_ROWS = 512
MERGE_ROWS = 512
ATT_TILE = 1024
RET_ROWS = 2048
RET_CHUNK = 256


def _silu(t):
    return t * (1.0 / (1.0 + jnp.exp(-t)))


def _rms(x):
    return x * lax.rsqrt(jnp.mean(x * x, axis=-1, keepdims=True) + EPS)


def _rope_block(blk, c, s_up, s_dn, shift):
    return (blk * c + pltpu.roll(blk, LANES - shift, 1) * s_up
            + pltpu.roll(blk, shift, 1) * s_dn)


def _inproj_kernel(x_ref, g_ref, wa_ref, wq_ref, wb_ref, atab_ref, rtab_ref, cw_ref,
                   aq_ref, ak_ref, av_ref, az_ref, c_ref,
                   rq_ref, rk_ref, rv_ref, rz_ref, ubuf_ref, *, tiles_per_seq):
    i = pl.program_id(0)
    tm = x_ref.shape[0]
    h = (_rms(x_ref[...]) * g_ref[...]).astype(BF16)

    def proj(gi):
        if gi < 8:
            w = wa_ref[:, gi * GROUP:(gi + 1) * GROUP]
        elif gi == 8:
            w = wq_ref[...]
        else:
            w = wb_ref[:, (gi - 9) * GROUP:(gi - 8) * GROUP]
        return jnp.dot(h, w, preferred_element_type=F32)

    c = atab_ref[:, 0:LANES]
    s_up = atab_ref[:, LANES:2 * LANES]
    s_dn = atab_ref[:, 2 * LANES:3 * LANES]
    for gi, out_ref, scale in ((0, aq_ref, (A_QK_DIM ** -0.5) * math.log2(math.e)),
                               (1, ak_ref, None)):
        y = proj(gi)
        for hh in range(A_HEADS):
            sl = slice(hh * LANES, (hh + 1) * LANES)
            rot = _rope_block(y[:, sl], c, s_up, s_dn, ROPE_DIM // 2)
            out_ref[:, sl] = (rot if scale is None else rot * scale).astype(BF16)
    av_ref[...] = proj(2).astype(BF16)
    az_ref[...] = _silu(proj(3)).astype(BF16)

    u = proj(4) * proj(6)

    @pl.when(i % tiles_per_seq == 0)
    def _():
        ubuf_ref[0:8, :] = jnp.zeros((8, GROUP), F32)

    ubuf_ref[8:8 + tm, :] = u
    conv = (cw_ref[0:1, :] * ubuf_ref[6:6 + tm, :] + cw_ref[1:2, :] * ubuf_ref[7:7 + tm, :]
            + cw_ref[2:3, :] * u)
    ubuf_ref[0:8, :] = u[tm - 8:tm, :]
    c_ref[...] = (proj(5) * conv * _silu(proj(7))).astype(BF16)

    y = proj(8)
    c = rtab_ref[:, 0:LANES]
    s_up = rtab_ref[:, LANES:2 * LANES]
    s_dn = rtab_ref[:, 2 * LANES:3 * LANES]
    lane = lax.broadcasted_iota(jnp.int32, (tm, LANES), 1)
    low = lane < R_QK_DIM
    for hh in range(R_HEADS):
        sl = slice(hh * LANES, (hh + 1) * LANES)
        rot = _rope_block(y[:, sl], c, s_up, s_dn, R_QK_DIM // 2)
        rq_ref[:, sl] = jnp.where(low, rot, 0.0).astype(BF16)
        rk_ref[:, sl] = jnp.where(low, pltpu.roll(rot, R_QK_DIM, 1), 0.0).astype(BF16)
    rv_ref[...] = proj(9).astype(BF16)
    rz_ref[...] = _silu(proj(10)).astype(BF16)


def _inproj(x2, norm_g, wa, wq, wb, atab, rtab, conv_w, seq):
    n, d = x2.shape
    tm = PROJ_ROWS
    tiles_per_seq = seq // tm
    row = lambda i: (i, 0)
    pos = lambda i: (i % tiles_per_seq, 0)
    const = lambda i: (0, 0)
    out_sds = jax.ShapeDtypeStruct((n, GROUP), BF16)
    out_spec = pl.BlockSpec((tm, GROUP), row)
    return pl.pallas_call(
        functools.partial(_inproj_kernel, tiles_per_seq=tiles_per_seq),
        grid=(n // tm,),
        in_specs=[
            pl.BlockSpec((tm, d), row),
            pl.BlockSpec((1, d), const),
            pl.BlockSpec(wa.shape, const),
            pl.BlockSpec(wq.shape, const),
            pl.BlockSpec(wb.shape, const),
            pl.BlockSpec((tm, atab.shape[1]), pos),
            pl.BlockSpec((tm, rtab.shape[1]), pos),
            pl.BlockSpec(conv_w.shape, const),
        ],
        out_specs=[out_spec] * 9,
        out_shape=[out_sds] * 9,
        scratch_shapes=[pltpu.VMEM((tm + 8, GROUP), F32)],
        compiler_params=pltpu.CompilerParams(
            dimension_semantics=("arbitrary",), vmem_limit_bytes=VMEM_LIMIT),
        name="inproj",
    )(x2, norm_g, wa, wq, wb, atab, rtab, conv_w)


def _attn_kernel(lam_ref, g_ref, q_ref, k_ref, v_ref, z_ref, o_ref,
                 qs_ref, m_ref, l_ref, acc_ref, *, lam_init):
    i = pl.program_id(2)
    tq = q_ref.shape[0]
    half = tq // 2
    lane = lax.broadcasted_iota(jnp.int32, (half, LANES), 1)
    zero = jnp.zeros((half, LANES), BF16)
    for part in range(2):
        qp = q_ref[part * half:(part + 1) * half, :]
        qs_ref[(2 * part) * half:(2 * part + 1) * half, :] = jnp.where(lane < A_QK_DIM, qp, zero)
        qs_ref[(2 * part + 1) * half:(2 * part + 2) * half, :] = jnp.where(
            lane >= A_QK_DIM, qp, zero)

    def scores(j, width, r0=0, nrows=2 * tq, diag_shift=None):
        keys = pl.ds(pl.multiple_of(j * tq, tq), width)
        s = lax.dot_general(qs_ref[r0:r0 + nrows, :], k_ref[keys, :], (((1,), (1,)), ((), ())),
                            preferred_element_type=F32)
        if diag_shift is not None:
            r = lax.broadcasted_iota(jnp.int32, s.shape, 0)
            cidx = lax.broadcasted_iota(jnp.int32, s.shape, 1)
            s = jnp.where(cidx <= diag_shift + jnp.where(r >= half, r - half, r), s, NEG_INF)
        return s

    def update(s, j, r0=0, first=False):
        nrows, width = s.shape
        keys = pl.ds(pl.multiple_of(j * tq, tq), width)
        rsl = slice(r0, r0 + nrows)
        m_new = jnp.broadcast_to(jnp.max(s, axis=1, keepdims=True), (nrows, LANES))
        if not first:
            m_prev = m_ref[rsl, :]
            m_new = jnp.maximum(m_prev, m_new)
            alpha = jnp.exp2(m_prev - m_new)
        m_ref[rsl, :] = m_new
        p = jnp.concatenate(
            [jnp.exp2(s[:, c * LANES:(c + 1) * LANES] - m_new).astype(BF16)
             for c in range(width // LANES)], axis=1)
        vs = v_ref[keys, :]
        v1 = jnp.concatenate([vs, jnp.ones_like(vs)], axis=1)
        for part in range(2):
            psl = slice(part * nrows // 2, (part + 1) * nrows // 2)
            osl = slice(r0 + part * nrows // 2, r0 + (part + 1) * nrows // 2)
            res = jnp.dot(p[psl], v1, preferred_element_type=F32)
            if first:
                acc_ref[osl, :] = res[:, :LANES]
                l_ref[osl, :] = res[:, LANES:]
            else:
                acc_ref[osl, :] = alpha[psl] * acc_ref[osl, :] + res[:, :LANES]
                l_ref[osl, :] = alpha[psl] * l_ref[osl, :] + res[:, LANES:]

    s_lo = scores(i, half, r0=0, nrows=tq, diag_shift=0)
    s_hi = scores(i, tq, r0=tq, nrows=tq, diag_shift=half)
    update(s_lo, i, r0=0, first=True)
    update(s_hi, i, r0=tq, first=True)

    def wide(kk, carry):
        update(scores(2 * kk, 2 * tq), 2 * kk)
        return carry

    lax.fori_loop(0, i // 2, wide, 0)

    @pl.when(i % 2 == 1)
    def _():
        update(scores(i - 1, tq), i - 1)

    o = acc_ref[...] / l_ref[...]
    al = lam_ref[...]
    lam = (jnp.exp(jnp.sum(al[0:1] * al[1:2], axis=1, keepdims=True))
           - jnp.exp(jnp.sum(al[2:3] * al[3:4], axis=1, keepdims=True)) + lam_init)
    d = jnp.concatenate([o[0:half] - lam * o[half:tq],
                         o[tq:tq + half] - lam * o[tq + half:2 * tq]], axis=0)
    d = _rms(d) * g_ref[...] * (1.0 - lam_init)
    o_ref[...] = (d * z_ref[...].astype(F32)).astype(BF16)


def _attention(aq, ak, av, az, attn_lambda, subln_g, lam_init):
    b, s, _ = aq.shape
    tq = ATT_TILE
    tile = pl.BlockSpec((None, tq, LANES), lambda bi, hi, qi: (bi, qi, hi))
    full = pl.BlockSpec((None, s, LANES), lambda bi, hi, qi: (bi, 0, hi))
    const = lambda bi, hi, qi: (0, 0)
    return pl.pallas_call(
        functools.partial(_attn_kernel, lam_init=lam_init),
        grid=(b, A_HEADS, s // tq),
        in_specs=[
            pl.BlockSpec(attn_lambda.shape, const),
            pl.BlockSpec(subln_g.shape, const),
            tile, full, full, tile,
        ],
        out_specs=tile,
        out_shape=jax.ShapeDtypeStruct(aq.shape, BF16),
        scratch_shapes=[pltpu.VMEM((2 * tq, LANES), BF16),
                        pltpu.VMEM((2 * tq, LANES), F32), pltpu.VMEM((2 * tq, LANES), F32),
                        pltpu.VMEM((2 * tq, LANES), F32)],
        compiler_params=pltpu.CompilerParams(
            dimension_semantics=("arbitrary", "arbitrary", "arbitrary"),
            vmem_limit_bytes=VMEM_LIMIT),
        name="diff_attention",
    )(attn_lambda, subln_g, aq, ak, av, az)


def _ret_kernel(q_ref, k_ref, v_ref, z_ref, dm_ref, xi_ref, zeta_ref, dec_ref, o_ref, st_ref):
    @pl.when(pl.program_id(1) == 0)
    def _():
        st_ref[...] = jnp.zeros(st_ref.shape, F32)

    chunk = dm_ref.shape[1]
    for hh in range(R_HEADS):
        sl = slice(hh * LANES, (hh + 1) * LANES)
        for cc in range(q_ref.shape[0] // chunk):
            rows = slice(cc * chunk, (cc + 1) * chunk)
            q = q_ref[rows, sl]
            k = k_ref[rows, sl]
            v = v_ref[rows, sl]
            s = lax.dot_general(q, k, (((1,), (1,)), ((), ())), preferred_element_type=F32)
            inner = jnp.dot((s * dm_ref[hh]).astype(BF16), v, preferred_element_type=F32)
            st = st_ref[hh]
            cross = jnp.dot(q, st.astype(BF16), preferred_element_type=F32) * xi_ref[hh]
            o = _rms(inner + cross)
            o_ref[rows, sl] = (o * z_ref[rows, sl].astype(F32)).astype(BF16)
            kz = (k.astype(F32) * zeta_ref[hh]).astype(BF16)
            kv = lax.dot_general(kz, v, (((0,), (0,)), ((), ())), preferred_element_type=F32)
            st_ref[hh] = dec_ref[hh] * st + kv


def _retention(rq, rk, rv, rz, dmask, xi, zeta, dec):
    b, s, w = rq.shape
    c = RET_ROWS
    tile = pl.BlockSpec((None, c, w), lambda bi, ti: (bi, ti, 0))
    const3 = lambda bi, ti: (0, 0, 0)
    return pl.pallas_call(
        _ret_kernel,
        grid=(b, s // c),
        in_specs=[tile, tile, tile, tile,
                  pl.BlockSpec(dmask.shape, const3), pl.BlockSpec(xi.shape, const3),
                  pl.BlockSpec(zeta.shape, const3), pl.BlockSpec(dec.shape, const3)],
        out_specs=tile,
        out_shape=jax.ShapeDtypeStruct(rq.shape, BF16),
        scratch_shapes=[pltpu.VMEM((R_HEADS, LANES, LANES), F32)],
        compiler_params=pltpu.CompilerParams(
            dimension_semantics=("arbitrary", "arbitrary"), vmem_limit_bytes=VMEM_LIMIT),
        name="retention",
    )(rq, rk, rv, rz, dmask, xi, zeta, dec)


def _merge_kernel(x_ref, a_ref, c_ref, r_ref, g_ref, wg_ref, wb_ref, wo_ref, fg_ref, o_ref,
                  *, final_norm):
    x = x_ref[...]
    d = x.shape[1]
    h = (_rms(x) * g_ref[...]).astype(BF16)
    merged = None
    for bi, br_ref in enumerate((a_ref, c_ref, r_ref)):
        t = jnp.dot(h, wg_ref[:, bi * d:(bi + 1) * d], preferred_element_type=F32)
        gate = 1.0 / (1.0 + jnp.exp(-t))
        contrib = gate * jnp.dot(br_ref[...], wb_ref[bi], preferred_element_type=F32)
        merged = contrib if merged is None else merged + contrib
    y = x + jnp.dot(merged.astype(BF16), wo_ref[...], preferred_element_type=F32)
    if final_norm:
        y = _rms(y) * fg_ref[...]
    o_ref[...] = y


def _merge(x2, a, c, r, norm_g, wg, wb, wo, final_g, final_norm):
    n, d = x2.shape
    tm = MERGE_ROWS
    row = lambda i: (i, 0)
    const = lambda i: (0, 0)
    br = pl.BlockSpec((tm, GROUP), row)
    return pl.pallas_call(
        functools.partial(_merge_kernel, final_norm=final_norm),
        grid=(n // tm,),
        in_specs=[pl.BlockSpec((tm, d), row), br, br, br,
                  pl.BlockSpec((1, d), const),
                  pl.BlockSpec(wg.shape, const),
                  pl.BlockSpec(wb.shape, lambda i: (0, 0, 0)),
                  pl.BlockSpec(wo.shape, const),
                  pl.BlockSpec((1, d), const)],
        out_specs=pl.BlockSpec((tm, d), row),
        out_shape=jax.ShapeDtypeStruct((n, d), F32),
        compiler_params=pltpu.CompilerParams(
            dimension_semantics=("arbitrary",), vmem_limit_bytes=VMEM_LIMIT),
        name="merge_outproj",
    )(x2, a, c, r, norm_g, wg, wb, wo, final_g)


def _attn_rope_table(seq):
    pos = jnp.arange(seq, dtype=F32)
    inv = ROPE_THETA ** (-jnp.arange(0, ROPE_DIM, 2, dtype=F32) / ROPE_DIM)
    ang = pos[:, None] * inv[None, :]
    half = ROPE_DIM // 2
    cos = jnp.tile(jnp.cos(ang), (1, LANES // half))
    sin = jnp.tile(jnp.sin(ang), (1, LANES // half))
    dd = (jnp.arange(LANES) % A_QK_DIM)[None, :]
    c = jnp.where(dd < ROPE_DIM, cos, 1.0)
    s_up = jnp.where(dd < half, -sin, 0.0)
    s_dn = jnp.where((dd >= half) & (dd < ROPE_DIM), sin, 0.0)
    return jnp.concatenate([c, s_up, s_dn], axis=1).astype(F32)


def _ret_rope_table(seq):
    pos = jnp.arange(seq, dtype=F32)
    inv = 1.0 / (RET_THETA ** jnp.linspace(0.0, 1.0, R_QK_DIM // 2, dtype=F32))
    ang = pos[:, None] * inv[None, :]
    half = R_QK_DIM // 2
    c = jnp.tile(jnp.cos(ang), (1, LANES // half))
    sin = jnp.tile(jnp.sin(ang), (1, LANES // half))
    dd = (jnp.arange(LANES) % R_QK_DIM)[None, :]
    s_up = jnp.where(dd < half, -sin, 0.0)
    s_dn = jnp.where(dd >= half, sin, 0.0)
    scale = jnp.where(jnp.arange(LANES) < R_QK_DIM, 1.0, R_QK_DIM ** -0.5)[None, :]
    return jnp.concatenate([c * scale, s_up * scale, s_dn * scale], axis=1).astype(F32)


def _ret_decay_tables(chunk):
    log_g = jnp.log(1.0 - 2.0 ** (-5.0 - jnp.arange(R_HEADS, dtype=F32)))
    idx = jnp.arange(chunk, dtype=F32)
    diff = idx[:, None] - idx[None, :]
    dmask = jnp.where(diff >= 0,
                      jnp.exp(jnp.where(diff >= 0, diff, 0.0)[None] * log_g[:, None, None]),
                      0.0)
    ones = jnp.ones((1, 1, LANES), F32)
    xi = jnp.exp((idx + 1.0)[None, :] * log_g[:, None])[:, :, None] * ones
    zeta = jnp.exp((chunk - 1 - idx)[None, :] * log_g[:, None])[:, :, None] * ones
    dec = jnp.exp(chunk * log_g)[:, None, None] * ones
    return dmask.astype(F32), xi.astype(F32), zeta.astype(F32), dec.astype(F32)


def _split_w_in(w_in_l):
    d = w_in_l.shape[0]
    g = GROUP
    mix = w_in_l[:, :8 * g]
    rq = w_in_l[:, 8 * g:8 * g + g // 2].reshape(d, R_HEADS, R_QK_DIM)
    rk = w_in_l[:, 8 * g + g // 2:9 * g].reshape(d, R_HEADS, R_QK_DIM)
    rqk = jnp.concatenate([rq, rk], axis=2).reshape(d, g)
    rest = w_in_l[:, 9 * g:11 * g]
    wg = w_in_l[:, 11 * g:].astype(BF16)
    return mix.astype(BF16), rqk.astype(BF16), rest.astype(BF16), wg


def kernel(x, norm_g, w_in, attn_lambda, attn_subln_g, conv_w, w_branch, w_out, final_norm_g):
    b, s, d = x.shape
    depth = w_in.shape[0]
    assert s % PROJ_ROWS == 0 and s % MERGE_ROWS == 0 and s % ATT_TILE == 0
    assert s % RET_ROWS == 0 and RET_ROWS % RET_CHUNK == 0 and ATT_TILE % (2 * LANES) == 0
    assert w_in.shape[2] == (11 + N_BRANCH * d // GROUP) * GROUP and d % LANES == 0
    x2 = x.reshape(b * s, d)
    atab = _attn_rope_table(s)
    rtab = _ret_rope_table(s)
    dmask, xi, zeta, dec = _ret_decay_tables(RET_CHUNK)
    final_g = final_norm_g.reshape(1, d)
    for layer in range(depth):
        lam_init = 0.8 - 0.6 * math.exp(-0.3 * layer)
        wa, wq, wb, wg = _split_w_in(w_in[layer])
        g = norm_g[layer].reshape(1, d)
        aq, ak, av, az, c, rq, rk, rv, rz = _inproj(x2, g, wa, wq, wb, atab, rtab,
                                                    conv_w[layer], s)
        sh = (b, s, GROUP)
        a = _attention(aq.reshape(sh), ak.reshape(sh), av.reshape(sh), az.reshape(sh),
                       attn_lambda[layer], attn_subln_g[layer].reshape(1, LANES), lam_init)
        r = _retention(rq.reshape(sh), rk.reshape(sh), rv.reshape(sh), rz.reshape(sh),
                       dmask, xi, zeta, dec)
        x2 = _merge(x2, a.reshape(b * s, GROUP), c, r.reshape(b * s, GROUP), g, wg,
                    w_branch[layer].astype(BF16), w_out[layer].astype(BF16), final_g,
                    layer == depth - 1)
    return x2.reshape(b, s, d)
```

```python
import functools
import math

import jax
import jax.numpy as jnp
from jax import lax
from jax.experimental import pallas as pl
from jax.experimental.pallas import tpu as pltpu

F32 = jnp.float32
BF16 = jnp.bfloat16

A_HEADS = 4
A_QK_DIM = 64
ROPE_THETA = 500000.0
ROPE_DIM = A_QK_DIM // 4
NEG_INF = -1e30
CONV_WIDTH = 3
R_HEADS = 4
R_QK_DIM = 64
RET_THETA = 10000.0
N_BRANCH = 3
EPS = 1e-6

LANES = 128
GROUP = 512
VMEM_LIMIT = 56 * 1024 * 1024

PROJ_ROWS = 512
MERGE_ROWS = 512
ATT_TILE = 1024
RET_ROWS = 2048
RET_CHUNK = 256


def _silu(t):
    return t * (0.5 * jnp.tanh(0.5 * t) + 0.5)


def _rms(x):
    return x * lax.rsqrt(jnp.mean(x * x, axis=-1, keepdims=True) + EPS)


def _rope_block(blk, c, s_up, s_dn, shift):
    return (blk * c + pltpu.roll(blk, LANES - shift, 1) * s_up
            + pltpu.roll(blk, shift, 1) * s_dn)


def _inproj_kernel(x_ref, g_ref, wa_ref, wq_ref, wb_ref, atab_ref, rtab_ref, cw_ref,
                   aq_ref, ak_ref, av_ref, az_ref, c_ref,
                   rq_ref, rk_ref, rv_ref, rz_ref, ubuf_ref, *, tiles_per_seq):
    i = pl.program_id(0)
    tm = x_ref.shape[0]
    h = (_rms(x_ref[...]) * g_ref[...]).astype(BF16)

    def proj(gi):
        if gi < 8:
            w = wa_ref[:, gi * GROUP:(gi + 1) * GROUP]
        elif gi == 8:
            w = wq_ref[...]
        else:
            w = wb_ref[:, (gi - 9) * GROUP:(gi - 8) * GROUP]
        return jnp.dot(h, w, preferred_element_type=F32)

    c = atab_ref[:, 0:LANES]
    s_up = atab_ref[:, LANES:2 * LANES]
    s_dn = atab_ref[:, 2 * LANES:3 * LANES]
    for gi, out_ref, scale in ((0, aq_ref, (A_QK_DIM ** -0.5) * math.log2(math.e)),
                               (1, ak_ref, None)):
        y = proj(gi)
        for hh in range(A_HEADS):
            sl = slice(hh * LANES, (hh + 1) * LANES)
            rot = _rope_block(y[:, sl], c, s_up, s_dn, ROPE_DIM // 2)
            out_ref[:, sl] = (rot if scale is None else rot * scale).astype(BF16)
    av_ref[...] = proj(2).astype(BF16)
    az_ref[...] = _silu(proj(3)).astype(BF16)

    u = proj(4) * proj(6)

    @pl.when(i % tiles_per_seq == 0)
    def _():
        ubuf_ref[0:8, :] = jnp.zeros((8, GROUP), F32)

    ubuf_ref[8:8 + tm, :] = u
    conv = (cw_ref[0:1, :] * ubuf_ref[6:6 + tm, :] + cw_ref[1:2, :] * ubuf_ref[7:7 + tm, :]
            + cw_ref[2:3, :] * u)
    ubuf_ref[0:8, :] = u[tm - 8:tm, :]
    c_ref[...] = (proj(5) * conv * _silu(proj(7))).astype(BF16)

    y = proj(8)
    c = rtab_ref[:, 0:LANES]
    s_up = rtab_ref[:, LANES:2 * LANES]
    s_dn = rtab_ref[:, 2 * LANES:3 * LANES]
    lane = lax.broadcasted_iota(jnp.int32, (tm, LANES), 1)
    low = lane < R_QK_DIM
    for hh in range(R_HEADS):
        sl = slice(hh * LANES, (hh + 1) * LANES)
        rot = _rope_block(y[:, sl], c, s_up, s_dn, R_QK_DIM // 2)
        rq_ref[:, sl] = jnp.where(low, rot, 0.0).astype(BF16)
        rk_ref[:, sl] = jnp.where(low, pltpu.roll(rot, R_QK_DIM, 1), 0.0).astype(BF16)
    rv_ref[...] = proj(9).astype(BF16)
    rz_ref[...] = _silu(proj(10)).astype(BF16)


def _inproj(x2, norm_g, wa, wq, wb, atab, rtab, conv_w, seq):
    n, d = x2.shape
    tm = PROJ_ROWS
    tiles_per_seq = seq // tm
    row = lambda i: (i, 0)
    pos = lambda i: (i % tiles_per_seq, 0)
    const = lambda i: (0, 0)
    out_sds = jax.ShapeDtypeStruct((n, GROUP), BF16)
    out_spec = pl.BlockSpec((tm, GROUP), row)
    return pl.pallas_call(
        functools.partial(_inproj_kernel, tiles_per_seq=tiles_per_seq),
        grid=(n // tm,),
        in_specs=[
            pl.BlockSpec((tm, d), row),
            pl.BlockSpec((1, d), const),
            pl.BlockSpec(wa.shape, const),
            pl.BlockSpec(wq.shape, const),
            pl.BlockSpec(wb.shape, const),
            pl.BlockSpec((tm, atab.shape[1]), pos),
            pl.BlockSpec((tm, rtab.shape[1]), pos),
            pl.BlockSpec(conv_w.shape, const),
        ],
        out_specs=[out_spec] * 9,
        out_shape=[out_sds] * 9,
        scratch_shapes=[pltpu.VMEM((tm + 8, GROUP), F32)],
        compiler_params=pltpu.CompilerParams(
            dimension_semantics=("arbitrary",), vmem_limit_bytes=VMEM_LIMIT),
        name="inproj",
    )(x2, norm_g, wa, wq, wb, atab, rtab, conv_w)


def _attn_kernel(lam_ref, g_ref, q_ref, k_ref, v_ref, z_ref, o_ref,
                 qs_ref, m_ref, l_ref, acc_ref, *, lam_init):
    i = pl.program_id(2)
    tq = q_ref.shape[0]
    half = tq // 2
    lane = lax.broadcasted_iota(jnp.int32, (half, LANES), 1)
    zero = jnp.zeros((half, LANES), BF16)
    for part in range(2):
        qp = q_ref[part * half:(part + 1) * half, :]
        qs_ref[(2 * part) * half:(2 * part + 1) * half, :] = jnp.where(lane < A_QK_DIM, qp, zero)
        qs_ref[(2 * part + 1) * half:(2 * part + 2) * half, :] = jnp.where(
            lane >= A_QK_DIM, qp, zero)

    def scores(j, width, r0=0, nrows=2 * tq, diag_shift=None):
        keys = pl.ds(pl.multiple_of(j * tq, tq), width)
        s = lax.dot_general(qs_ref[r0:r0 + nrows, :], k_ref[keys, :], (((1,), (1,)), ((), ())),
                            preferred_element_type=F32)
        if diag_shift is not None:
            r = lax.broadcasted_iota(jnp.int32, s.shape, 0)
            cidx = lax.broadcasted_iota(jnp.int32, s.shape, 1)
            s = jnp.where(cidx <= diag_shift + jnp.where(r >= half, r - half, r), s, NEG_INF)
        return s

    def update(s, j, r0=0, first=False):
        nrows, width = s.shape
        keys = pl.ds(pl.multiple_of(j * tq, tq), width)
        rsl = slice(r0, r0 + nrows)
        m_new = jnp.broadcast_to(jnp.max(s, axis=1, keepdims=True), (nrows, LANES))
        if not first:
            m_prev = m_ref[rsl, :]
            m_new = jnp.maximum(m_prev, m_new)
            alpha = jnp.exp2(m_prev - m_new)
        m_ref[rsl, :] = m_new
        p = jnp.concatenate(
            [jnp.exp2(s[:, c * LANES:(c + 1) * LANES] - m_new).astype(BF16)
             for c in range(width // LANES)], axis=1)
        vs = v_ref[keys, :]
        v1 = jnp.concatenate([vs, jnp.ones_like(vs)], axis=1)
        for part in range(2):
            psl = slice(part * nrows // 2, (part + 1) * nrows // 2)
            osl = slice(r0 + part * nrows // 2, r0 + (part + 1) * nrows // 2)
            res = jnp.dot(p[psl], v1, preferred_element_type=F32)
            if first:
                acc_ref[osl, :] = res[:, :LANES]
                l_ref[osl, :] = res[:, LANES:]
            else:
                acc_ref[osl, :] = alpha[psl] * acc_ref[osl, :] + res[:, :LANES]
                l_ref[osl, :] = alpha[psl] * l_ref[osl, :] + res[:, LANES:]

    s_lo = scores(i, half, r0=0, nrows=tq, diag_shift=0)
    s_hi = scores(i, tq, r0=tq, nrows=tq, diag_shift=half)
    update(s_lo, i, r0=0, first=True)
    update(s_hi, i, r0=tq, first=True)

    def wide(kk, carry):
        update(scores(2 * kk, 2 * tq), 2 * kk)
        return carry

    lax.fori_loop(0, i // 2, wide, 0)

    @pl.when(i % 2 == 1)
    def _():
        update(scores(i - 1, tq), i - 1)

    o = acc_ref[...] / l_ref[...]
    al = lam_ref[...]
    lam = (jnp.exp(jnp.sum(al[0:1] * al[1:2], axis=1, keepdims=True))
           - jnp.exp(jnp.sum(al[2:3] * al[3:4], axis=1, keepdims=True)) + lam_init)
    d = jnp.concatenate([o[0:half] - lam * o[half:tq],
                         o[tq:tq + half] - lam * o[tq + half:2 * tq]], axis=0)
    d = _rms(d) * g_ref[...] * (1.0 - lam_init)
    o_ref[...] = (d * z_ref[...].astype(F32)).astype(BF16)


def _attention(aq, ak, av, az, attn_lambda, subln_g, lam_init):
    b, s, _ = aq.shape
    tq = ATT_TILE
    tile = pl.BlockSpec((None, tq, LANES), lambda bi, hi, qi: (bi, qi, hi))
    full = pl.BlockSpec((None, s, LANES), lambda bi, hi, qi: (bi, 0, hi))
    const = lambda bi, hi, qi: (0, 0)
    return pl.pallas_call(
        functools.partial(_attn_kernel, lam_init=lam_init),
        grid=(b, A_HEADS, s // tq),
        in_specs=[
            pl.BlockSpec(attn_lambda.shape, const),
            pl.BlockSpec(subln_g.shape, const),
            tile, full, full, tile,
        ],
        out_specs=tile,
        out_shape=jax.ShapeDtypeStruct(aq.shape, BF16),
        scratch_shapes=[pltpu.VMEM((2 * tq, LANES), BF16),
                        pltpu.VMEM((2 * tq, LANES), F32), pltpu.VMEM((2 * tq, LANES), F32),
                        pltpu.VMEM((2 * tq, LANES), F32)],
        compiler_params=pltpu.CompilerParams(
            dimension_semantics=("arbitrary", "arbitrary", "arbitrary"),
            vmem_limit_bytes=VMEM_LIMIT),
        name="diff_attention",
    )(attn_lambda, subln_g, aq, ak, av, az)


def _ret_kernel(q_ref, k_ref, v_ref, z_ref, dm_ref, xi_ref, zeta_ref, dec_ref, o_ref, st_ref):
    @pl.when(pl.program_id(1) == 0)
    def _():
        st_ref[...] = jnp.zeros(st_ref.shape, F32)

    chunk = dm_ref.shape[1]
    for hh in range(R_HEADS):
        sl = slice(hh * LANES, (hh + 1) * LANES)
        for cc in range(q_ref.shape[0] // chunk):
            rows = slice(cc * chunk, (cc + 1) * chunk)
            q = q_ref[rows, sl]
            k = k_ref[rows, sl]
            v = v_ref[rows, sl]
            s = lax.dot_general(q, k, (((1,), (1,)), ((), ())), preferred_element_type=F32)
            inner = jnp.dot((s * dm_ref[hh]).astype(BF16), v, preferred_element_type=F32)
            st = st_ref[hh]
            cross = jnp.dot(q, st.astype(BF16), preferred_element_type=F32) * xi_ref[hh]
            o = _rms(inner + cross)
            o_ref[rows, sl] = (o * z_ref[rows, sl].astype(F32)).astype(BF16)
            kz = (k.astype(F32) * zeta_ref[hh]).astype(BF16)
            kv = lax.dot_general(kz, v, (((0,), (0,)), ((), ())), preferred_element_type=F32)
            st_ref[hh] = dec_ref[hh] * st + kv


def _retention(rq, rk, rv, rz, dmask, xi, zeta, dec):
    b, s, w = rq.shape
    c = RET_ROWS
    tile = pl.BlockSpec((None, c, w), lambda bi, ti: (bi, ti, 0))
    const3 = lambda bi, ti: (0, 0, 0)
    return pl.pallas_call(
        _ret_kernel,
        grid=(b, s // c),
        in_specs=[tile, tile, tile, tile,
                  pl.BlockSpec(dmask.shape, const3), pl.BlockSpec(xi.shape, const3),
                  pl.BlockSpec(zeta.shape, const3), pl.BlockSpec(dec.shape, const3)],
        out_specs=tile,
        out_shape=jax.ShapeDtypeStruct(rq.shape, BF16),
        scratch_shapes=[pltpu.VMEM((R_HEADS, LANES, LANES), F32)],
        compiler_params=pltpu.CompilerParams(
            dimension_semantics=("arbitrary", "arbitrary"), vmem_limit_bytes=VMEM_LIMIT),
        name="retention",
    )(rq, rk, rv, rz, dmask, xi, zeta, dec)


def _merge_kernel(x_ref, a_ref, c_ref, r_ref, g_ref, wg_ref, wb_ref, wo_ref, fg_ref, o_ref,
                  *, final_norm):
    x = x_ref[...]
    d = x.shape[1]
    h = (_rms(x) * g_ref[...]).astype(BF16)
    merged = None
    for bi, br_ref in enumerate((a_ref, c_ref, r_ref)):
        t = jnp.dot(h, wg_ref[:, bi * d:(bi + 1) * d], preferred_element_type=F32)
        gate = 0.5 * jnp.tanh(0.5 * t) + 0.5
        contrib = gate * jnp.dot(br_ref[...], wb_ref[bi], preferred_element_type=F32)
        merged = contrib if merged is None else merged + contrib
    y = x + jnp.dot(merged.astype(BF16), wo_ref[...], preferred_element_type=F32)
    if final_norm:
        y = _rms(y) * fg_ref[...]
    o_ref[...] = y


def _merge(x2, a, c, r, norm_g, wg, wb, wo, final_g, final_norm):
    n, d = x2.shape
    tm = MERGE_ROWS
    row = lambda i: (i, 0)
    const = lambda i: (0, 0)
    br = pl.BlockSpec((tm, GROUP), row)
    return pl.pallas_call(
        functools.partial(_merge_kernel, final_norm=final_norm),
        grid=(n // tm,),
        in_specs=[pl.BlockSpec((tm, d), row), br, br, br,
                  pl.BlockSpec((1, d), const),
                  pl.BlockSpec(wg.shape, const),
                  pl.BlockSpec(wb.shape, lambda i: (0, 0, 0)),
                  pl.BlockSpec(wo.shape, const),
                  pl.BlockSpec((1, d), const)],
        out_specs=pl.BlockSpec((tm, d), row),
        out_shape=jax.ShapeDtypeStruct((n, d), F32),
        compiler_params=pltpu.CompilerParams(
            dimension_semantics=("arbitrary",), vmem_limit_bytes=VMEM_LIMIT),
        name="merge_outproj",
    )(x2, a, c, r, norm_g, wg, wb, wo, final_g)


def _attn_rope_table(seq):
    pos = jnp.arange(seq, dtype=F32)
    inv = ROPE_THETA ** (-jnp.arange(0, ROPE_DIM, 2, dtype=F32) / ROPE_DIM)
    ang = pos[:, None] * inv[None, :]
    half = ROPE_DIM // 2
    cos = jnp.tile(jnp.cos(ang), (1, LANES // half))
    sin = jnp.tile(jnp.sin(ang), (1, LANES // half))
    dd = (jnp.arange(LANES) % A_QK_DIM)[None, :]
    c = jnp.where(dd < ROPE_DIM, cos, 1.0)
    s_up = jnp.where(dd < half, -sin, 0.0)
    s_dn = jnp.where((dd >= half) & (dd < ROPE_DIM), sin, 0.0)
    return jnp.concatenate([c, s_up, s_dn], axis=1).astype(F32)


def _ret_rope_table(seq):
    pos = jnp.arange(seq, dtype=F32)
    inv = 1.0 / (RET_THETA ** jnp.linspace(0.0, 1.0, R_QK_DIM // 2, dtype=F32))
    ang = pos[:, None] * inv[None, :]
    half = R_QK_DIM // 2
    c = jnp.tile(jnp.cos(ang), (1, LANES // half))
    sin = jnp.tile(jnp.sin(ang), (1, LANES // half))
    dd = (jnp.arange(LANES) % R_QK_DIM)[None, :]
    s_up = jnp.where(dd < half, -sin, 0.0)
    s_dn = jnp.where(dd >= half, sin, 0.0)
    scale = jnp.where(jnp.arange(LANES) < R_QK_DIM, 1.0, R_QK_DIM ** -0.5)[None, :]
    return jnp.concatenate([c * scale, s_up * scale, s_dn * scale], axis=1).astype(F32)


def _ret_decay_tables(chunk):
    log_g = jnp.log(1.0 - 2.0 ** (-5.0 - jnp.arange(R_HEADS, dtype=F32)))
    idx = jnp.arange(chunk, dtype=F32)
    diff = idx[:, None] - idx[None, :]
    dmask = jnp.where(diff >= 0,
                      jnp.exp(jnp.where(diff >= 0, diff, 0.0)[None] * log_g[:, None, None]),
                      0.0)
    ones = jnp.ones((1, 1, LANES), F32)
    xi = jnp.exp((idx + 1.0)[None, :] * log_g[:, None])[:, :, None] * ones
    zeta = jnp.exp((chunk - 1 - idx)[None, :] * log_g[:, None])[:, :, None] * ones
    dec = jnp.exp(chunk * log_g)[:, None, None] * ones
    return dmask.astype(F32), xi.astype(F32), zeta.astype(F32), dec.astype(F32)


def _split_w_in(w_in_l):
    d = w_in_l.shape[0]
    g = GROUP
    mix = w_in_l[:, :8 * g]
    rq = w_in_l[:, 8 * g:8 * g + g // 2].reshape(d, R_HEADS, R_QK_DIM)
    rk = w_in_l[:, 8 * g + g // 2:9 * g].reshape(d, R_HEADS, R_QK_DIM)
    rqk = jnp.concatenate([rq, rk], axis=2).reshape(d, g)
    rest = w_in_l[:, 9 * g:11 * g]
    wg = w_in_l[:, 11 * g:].astype(BF16)
    return mix.astype(BF16), rqk.astype(BF16), rest.astype(BF16), wg


def kernel(x, norm_g, w_in, attn_lambda, attn_subln_g, conv_w, w_branch, w_out, final_norm_g):
    b, s, d = x.shape
    depth = w_in.shape[0]
    assert s % PROJ_ROWS == 0 and s % MERGE_ROWS == 0 and s % ATT_TILE == 0
    assert s % RET_ROWS == 0 and RET_ROWS % RET_CHUNK == 0 and ATT_TILE % (2 * LANES) == 0
    assert w_in.shape[2] == (11 + N_BRANCH * d // GROUP) * GROUP and d % LANES == 0
    x2 = x.reshape(b * s, d)
    atab = _attn_rope_table(s)
    rtab = _ret_rope_table(s)
    dmask, xi, zeta, dec = _ret_decay_tables(RET_CHUNK)
    final_g = final_norm_g.reshape(1, d)
    for layer in range(depth):
        lam_init = 0.8 - 0.6 * math.exp(-0.3 * layer)
        wa, wq, wb, wg = _split_w_in(w_in[layer])
        g = norm_g[layer].reshape(1, d)
        aq, ak, av, az, c, rq, rk, rv, rz = _inproj(x2, g, wa, wq, wb, atab, rtab,
                                                    conv_w[layer], s)
        sh = (b, s, GROUP)
        a = _attention(aq.reshape(sh), ak.reshape(sh), av.reshape(sh), az.reshape(sh),
                       attn_lambda[layer], attn_subln_g[layer].reshape(1, LANES), lam_init)
        r = _retention(rq.reshape(sh), rk.reshape(sh), rv.reshape(sh), rz.reshape(sh),
                       dmask, xi, zeta, dec)
        x2 = _merge(x2, a.reshape(b * s, GROUP), c, r.reshape(b * s, GROUP), g, wg,
                    w_branch[layer].astype(BF16), w_out[layer].astype(BF16), final_g,
                    layer == depth - 1)
    return x2.reshape(b, s, d)
```
